```python
import math
import jax, jax.numpy as jnp
from jax import lax
import numpy as np

D_MODEL = 1024
BATCH = 32
SEQ = 2048
DEPTH = 4

CHUNK = 64
H_A = 8
DH_A = 64
D_CQ = 256
D_C = 128
H_IDX = 4
D_IDX = 64
TOPK_MAX = 256
QB_A = 32
GMLP_CHUNK = 128
G_B = 8
GMLP_WIDTH = 512
CG_B = GMLP_WIDTH // G_B
H_C = 8
DH_C = 64
QB_C = 128
QK_W_C = H_C * 2 * DH_C
N_EXPERTS = 32
TOP_K = 4
D_FF = 1024
SWIGLU_ALPHA = 1.702
SWIGLU_LIMIT = 7.0
MOE_BLOCK = 256
DN_ALPHA = (2 * DEPTH) ** 0.25
DN_BETA = (8 * DEPTH) ** -0.25
N_EVEN = (DEPTH + 1) // 2
N_ODD = DEPTH // 2
E_IN_EVEN = D_CQ + D_C + D_IDX + H_IDX + 2 * GMLP_WIDTH
MIX_EVEN = H_A * DH_A + GMLP_WIDTH
LN_EPS = 1e-5
NEG = -1e30

kernel_name = 'hybrid_dsa_gmlp_diffattn_moe_deepnorm'


def _layer_norm(x, g, b):
    xf = x.astype(jnp.float32)
    mu = jnp.mean(xf, -1, keepdims=True)
    var = jnp.mean(jnp.square(xf - mu), -1, keepdims=True)
    return ((xf - mu) * lax.rsqrt(var + LN_EPS) * g + b).astype(x.dtype)


def _rms_norm(x, g):
    xf = x.astype(jnp.float32)
    return (xf * lax.rsqrt(jnp.mean(xf * xf, -1, keepdims=True) + LN_EPS) * g).astype(x.dtype)


def _alibi_slopes(n):
    return jnp.exp2(-8.0 * jnp.arange(1, n + 1, dtype=jnp.float32) / n)


def _to_blocks(t, qb):
    b, s = t.shape[:2]
    return jnp.moveaxis(t.reshape((b, s // qb, qb) + t.shape[2:]), 1, 0)


def _from_blocks(t):
    t = jnp.moveaxis(t, 0, 1)
    return t.reshape((t.shape[0], t.shape[1] * t.shape[2]) + t.shape[3:])


def _dsa_attention(c_q, c_kv, k_idx, w_idx, w_uq, w_uk, w_uv, w_iq):
    f32 = jnp.float32
    b, s, _ = c_kv.shape
    n_sel = min(TOPK_MAX, s // 4)
    q = jnp.einsum('bsc,chd->bshd', c_q, w_uq)
    q_lat = jnp.einsum('bshd,hdc->bshc', q, w_uk)
    q_idx = jnp.einsum('bsc,chd->bshd', c_q, w_iq)
    w_idx = w_idx.astype(f32) * (H_IDX ** -0.5)
    slopes = _alibi_slopes(H_A)
    k_chunk = jnp.arange(s) // CHUNK

    def block(args):
        i, qi, ql, wi = args
        t_pos = i * QB_A + jnp.arange(QB_A)
        t_chunk = t_pos // CHUNK
        rel = jax.nn.relu(jnp.einsum('bqhd,bsd->bqhs', qi, k_idx).astype(f32) * (D_IDX ** -0.5))
        score = jnp.einsum('bqhs,bqh->bqs', rel, wi)
        score = jnp.where((k_chunk[None, :] <= t_chunk[:, None])[None], score, NEG)
        _, sel = lax.top_k(score, n_sel)
        c_sel = jax.vmap(lambda c, ix: c[ix])(c_kv, sel)
        logits = jnp.einsum('bqhc,bqkc->bqhk', ql, c_sel).astype(f32) * (DH_A ** -0.5)
        dist = jnp.abs(t_pos[None, :, None] - sel).astype(f32)
        logits = logits - slopes[None, None, :, None] * dist[:, :, None, :]
        valid = (sel // CHUNK) <= t_chunk[None, :, None]
        logits = jnp.where(valid[:, :, None, :], logits, NEG)
        p = jax.nn.softmax(logits, axis=-1).astype(c_kv.dtype)
        return jnp.einsum('bqhk,bqkc->bqhc', p, c_sel)

    o_lat = lax.map(block, (jnp.arange(s // QB_A), _to_blocks(q_idx, QB_A),
                            _to_blocks(q_lat, QB_A), _to_blocks(w_idx, QB_A)))
    o = jnp.einsum('bshc,hcd->bshd', _from_blocks(o_lat), w_uv)
    return o.reshape(b, s, H_A * DH_A)


def _spatial_gating(u, v, v_g, v_b, w_s, b_s):
    b, s, _ = u.shape
    u = jax.nn.gelu(u)
    v = _layer_norm(jax.nn.gelu(v), v_g, v_b)
    mask = jnp.tril(jnp.ones((GMLP_CHUNK, GMLP_CHUNK), dtype=bool))
    w = jnp.where(mask[None], w_s, 0.0)
    vg = v.reshape(b, s // GMLP_CHUNK, GMLP_CHUNK, G_B, CG_B)
    mixed = jnp.einsum('gij,bnjgc->bnigc', w, vg) + jnp.transpose(b_s)[None, None, :, :, None]
    return u * mixed.reshape(b, s, GMLP_WIDTH)


def _even_mixer(x, w_in, q_norm_g, kv_norm_g, w_uq, w_uk, w_uv, w_iq, idx_k_g, idx_k_b,
                v_norm_g, v_norm_b, w_s, b_s, w_out):
    proj = jnp.dot(x, w_in)
    o1 = D_CQ
    o2 = o1 + D_C
    o3 = o2 + D_IDX
    o4 = o3 + H_IDX
    o5 = o4 + GMLP_WIDTH
    c_q = _rms_norm(proj[..., :o1], q_norm_g)
    c_kv = _rms_norm(proj[..., o1:o2], kv_norm_g)
    k_idx = _layer_norm(proj[..., o2:o3], idx_k_g, idx_k_b)
    w_idx = proj[..., o3:o4]
    o_a = _dsa_attention(c_q, c_kv, k_idx, w_idx, w_uq, w_uk, w_uv, w_iq)
    o_b = _spatial_gating(proj[..., o4:o5], proj[..., o5:], v_norm_g, v_norm_b, w_s, b_s)
    return jnp.dot(jnp.concatenate([o_a, o_b], axis=-1), w_out)


def _diff_attention(x, w_in, lq1, lk1, lq2, lk2, subln_g, w_out, lam_init):
    f32 = jnp.float32
    b, s, _ = x.shape
    proj = jnp.dot(x, w_in)
    q = proj[..., :QK_W_C].reshape(b, s, H_C, 2, DH_C)
    k = proj[..., QK_W_C:2 * QK_W_C].reshape(b, s, H_C, 2, DH_C)
    v = proj[..., 2 * QK_W_C:].reshape(b, s, H_C, 2 * DH_C)
    lam = (jnp.exp(jnp.sum(lq1.astype(f32) * lk1.astype(f32)))
           - jnp.exp(jnp.sum(lq2.astype(f32) * lk2.astype(f32))) + lam_init)
    slopes = _alibi_slopes(H_C)
    k_pos = jnp.arange(s)
    k_chunk = k_pos // CHUNK

    def block(args):
        i, qb = args
        t_pos = i * QB_C + jnp.arange(QB_C)
        logits = jnp.einsum('bqhmd,bshmd->bhmqs', qb, k).astype(f32) * (DH_C ** -0.5)
        dist = jnp.abs(t_pos[:, None] - k_pos[None, :]).astype(f32)
        logits = logits - slopes[:, None, None, None] * dist
        allowed = k_chunk[None, :] <= (t_pos // CHUNK)[:, None]
        p = jax.nn.softmax(jnp.where(allowed, logits, NEG), axis=-1)
        attn = (p[:, :, 0] - lam * p[:, :, 1]).astype(v.dtype)
        return jnp.einsum('bhqs,bshe->bqhe', attn, v)

    o = _from_blocks(lax.map(block, (jnp.arange(s // QB_C), _to_blocks(q, QB_C))))
    o = _rms_norm(o, subln_g) * (1.0 - lam_init)
    return jnp.dot(o.reshape(b, s, H_C * 2 * DH_C), w_out)


def _clamped_swiglu(gu):
    gate, up = gu[..., :D_FF], gu[..., D_FF:]
    gate = jnp.minimum(gate, SWIGLU_LIMIT)
    up = jnp.clip(up, -SWIGLU_LIMIT, SWIGLU_LIMIT)
    return (up + 1.0) * (gate * jax.nn.sigmoid(gate * SWIGLU_ALPHA))


def _moe_ffn(h, router_w, router_b, w_gu, b_gu, w_dn, b_dn):
    n_tok = h.shape[0]
    logits = jnp.dot(h, router_w).astype(jnp.float32) + router_b.astype(jnp.float32)
    top_logit, top_e = lax.top_k(logits, TOP_K)
    gate = jax.nn.softmax(top_logit, axis=-1).astype(h.dtype)
    n_asg = n_tok * TOP_K
    flat_e = top_e.reshape(n_asg)
    flat_tok = jnp.arange(n_asg, dtype=jnp.int32) // TOP_K
    flat_gate = gate.reshape(n_asg)
    order = jnp.argsort(flat_e)
    sorted_e = flat_e[order]
    counts = jnp.bincount(flat_e, length=N_EXPERTS)
    padded = (counts + MOE_BLOCK - 1) // MOE_BLOCK * MOE_BLOCK
    pad_end = jnp.cumsum(padded)
    pad_start = pad_end - padded
    start = jnp.cumsum(counts) - counts
    dest = pad_start[sorted_e] + jnp.arange(n_asg, dtype=jnp.int32) - start[sorted_e]
    n_blocks = -(-n_asg // MOE_BLOCK) + N_EXPERTS
    n_rows = n_blocks * MOE_BLOCK
    row_tok = jnp.zeros((n_rows,), jnp.int32).at[dest].set(flat_tok[order])
    row_gate = jnp.zeros((n_rows,), h.dtype).at[dest].set(flat_gate[order])
    block_e = jnp.minimum(jnp.searchsorted(pad_end, jnp.arange(n_blocks, dtype=jnp.int32) * MOE_BLOCK,
                                           side='right'), N_EXPERTS - 1)

    def step(out, blk):
        tok, g, e = blk
        xb = h[tok]
        y = jnp.dot(_clamped_swiglu(jnp.dot(xb, w_gu[e]) + b_gu[e]), w_dn[e]) + b_dn[e]
        return out.at[tok].add(g[:, None] * y), None

    out, _ = lax.scan(step, jnp.zeros_like(h),
                      (row_tok.reshape(n_blocks, MOE_BLOCK), row_gate.reshape(n_blocks, MOE_BLOCK), block_e))
    return out


def setup_inputs(seed: int = 0) -> dict:
    key = jax.random.key(seed)
    k = jax.random.split(key, 32)
    f32 = jnp.float32
    D = D_MODEL

    def nrm(kk, shape, scale):
        return jax.random.normal(kk, shape, f32) * scale

    def gain(kk, shape):
        return 1.0 + nrm(kk, shape, 0.05)

    NE, NO, L = N_EVEN, N_ODD, DEPTH
    od_w_in = nrm(k[15], (NO, D, 3 * QK_W_C), D ** -0.5)
    od_w_in = od_w_in.at[..., 2 * QK_W_C:].multiply(DN_BETA)
    return {
        'x': nrm(k[0], (BATCH, SEQ, D), 1.0),
        'ev_w_in': nrm(k[1], (NE, D, E_IN_EVEN), D ** -0.5),
        'ev_q_norm': gain(k[2], (NE, D_CQ)),
        'ev_kv_norm': gain(k[3], (NE, D_C)),
        'ev_w_uq': nrm(k[4], (NE, D_CQ, H_A, DH_A), D_CQ ** -0.5),
        'ev_w_uk': nrm(k[5], (NE, H_A, DH_A, D_C), DH_A ** -0.5),
        'ev_w_uv': nrm(k[6], (NE, H_A, D_C, DH_A), D_C ** -0.5 * DN_BETA),
        'ev_w_iq': nrm(k[7], (NE, D_CQ, H_IDX, D_IDX), D_CQ ** -0.5),
        'ev_idx_k_g': gain(k[8], (NE, D_IDX)),
        'ev_idx_k_b': nrm(k[9], (NE, D_IDX), 0.02),
        'ev_v_norm_g': gain(k[10], (NE, GMLP_WIDTH)),
        'ev_v_norm_b': nrm(k[11], (NE, GMLP_WIDTH), 0.02),
        'ev_w_s': nrm(k[12], (NE, G_B, GMLP_CHUNK, GMLP_CHUNK), GMLP_CHUNK ** -0.5),
        'ev_b_s': 1.0 + nrm(k[13], (NE, G_B, GMLP_CHUNK), 0.1),
        'ev_w_out': nrm(k[14], (NE, MIX_EVEN, D), MIX_EVEN ** -0.5 * DN_BETA),
        'od_w_in': od_w_in,
        'od_lambda_q1': nrm(k[16], (NO, DH_C), 0.1),
        'od_lambda_k1': nrm(k[17], (NO, DH_C), 0.1),
        'od_lambda_q2': nrm(k[18], (NO, DH_C), 0.1),
        'od_lambda_k2': nrm(k[19], (NO, DH_C), 0.1),
        'od_subln_g': gain(k[20], (NO, 2 * DH_C)),
        'od_w_out': nrm(k[21], (NO, QK_W_C, D), QK_W_C ** -0.5 * DN_BETA),
        'ln1_g': gain(k[22], (L, D)),
        'ln1_b': nrm(k[23], (L, D), 0.02),
        'ln2_g': gain(k[24], (L, D)),
        'ln2_b': nrm(k[25], (L, D), 0.02),
        'router_w': nrm(k[26], (L, D, N_EXPERTS), D ** -0.5),
        'router_b': nrm(k[27], (L, N_EXPERTS), 0.01),
        'exp_w_gu': nrm(k[28], (L, N_EXPERTS, D, 2 * D_FF), D ** -0.5 * DN_BETA),
        'exp_b_gu': nrm(k[29], (L, N_EXPERTS, 2 * D_FF), 0.01),
        'exp_w_dn': nrm(k[30], (L, N_EXPERTS, D_FF, D), D_FF ** -0.5 * DN_BETA),
        'exp_b_dn': nrm(k[31], (L, N_EXPERTS, D), 0.01),
    }


def reference(x, ev_w_in, ev_q_norm, ev_kv_norm, ev_w_uq, ev_w_uk, ev_w_uv, ev_w_iq, ev_idx_k_g,
              ev_idx_k_b, ev_v_norm_g, ev_v_norm_b, ev_w_s, ev_b_s, ev_w_out, od_w_in, od_lambda_q1,
              od_lambda_k1, od_lambda_q2, od_lambda_k2, od_subln_g, od_w_out, ln1_g, ln1_b, ln2_g,
              ln2_b, router_w, router_b, exp_w_gu, exp_b_gu, exp_w_dn, exp_b_dn):
    b, s, d = x.shape
    h = x
    for l in range(DEPTH):
        j = l // 2
        if l % 2 == 0:
            mix = _even_mixer(h, ev_w_in[j], ev_q_norm[j], ev_kv_norm[j], ev_w_uq[j], ev_w_uk[j],
                              ev_w_uv[j], ev_w_iq[j], ev_idx_k_g[j], ev_idx_k_b[j], ev_v_norm_g[j],
                              ev_v_norm_b[j], ev_w_s[j], ev_b_s[j], ev_w_out[j])
        else:
            lam_init = 0.8 - 0.6 * math.exp(-0.3 * l)
            mix = _diff_attention(h, od_w_in[j], od_lambda_q1[j], od_lambda_k1[j], od_lambda_q2[j],
                                  od_lambda_k2[j], od_subln_g[j], od_w_out[j], lam_init)
        h = _layer_norm(DN_ALPHA * h + mix, ln1_g[l], ln1_b[l])
        ff = _moe_ffn(h.reshape(b * s, d), router_w[l], router_b[l], exp_w_gu[l], exp_b_gu[l],
                      exp_w_dn[l], exp_b_dn[l]).reshape(b, s, d)
        h = _layer_norm(DN_ALPHA * h + ff, ln2_g[l], ln2_b[l])
    return h
```

```python
import functools
import math

import jax
import jax.numpy as jnp
from jax import lax
from jax.experimental import pallas as pl
from jax.experimental.pallas import tpu as pltpu

F32 = jnp.float32
BF16 = jnp.bfloat16
I32 = jnp.int32

D_MODEL = 1024
DEPTH = 4
CHUNK = 64
CHUNK_SHIFT = 6
H_A = 8
DH_A = 64
D_CQ = 256
D_C = 128
H_IDX = 4
D_IDX = 64
TOPK_MAX = 256
GMLP_CHUNK = 128
G_B = 8
GMLP_WIDTH = 512
CG_B = GMLP_WIDTH // G_B
H_C = 8
DH_C = 64
QK_W_C = H_C * 2 * DH_C
N_EXPERTS = 32
TOP_K = 4
D_FF = 1024
SWIGLU_ALPHA = 1.702
SWIGLU_LIMIT = 7.0
DN_ALPHA = (2 * DEPTH) ** 0.25
E_IN_EVEN = D_CQ + D_C + D_IDX + H_IDX + 2 * GMLP_WIDTH
LN_EPS = 1e-5
NEG = -1e30

LANES = 128
E_IN_PAD = 1536
IDX_COLS = D_CQ + D_C
INT_MIN = -(2 ** 31)

PROJ_TM = 512
DSA_TQ = 256
DIFF_TQ = 512
GMLP_TG = 512
FFN_TM = 512
FFN_FC = 512
COMB_TM = 256
VMEM_LIMIT = 56 * 1024 * 1024


def _cparams(sem):
    return pltpu.CompilerParams(dimension_semantics=sem, vmem_limit_bytes=VMEM_LIMIT)


def _proj_kernel(x_ref, w_ref, o_ref):
    o_ref[...] = jnp.dot(x_ref[...].astype(BF16), w_ref[...],
                         preferred_element_type=F32).astype(o_ref.dtype)


def _proj(x2, w, out_dtype):
    m, k = x2.shape
    n = w.shape[1]
    tm = min(PROJ_TM, m)
    return pl.pallas_call(
        _proj_kernel,
        grid=(m // tm,),
        in_specs=[pl.BlockSpec((tm, k), lambda i: (i, 0)),
                  pl.BlockSpec((k, n), lambda i: (0, 0))],
        out_specs=pl.BlockSpec((tm, n), lambda i: (i, 0)),
        out_shape=jax.ShapeDtypeStruct((m, n), out_dtype),
        compiler_params=_cparams(("parallel",)),
        name="proj",
    )(x2, w)


def _dsa_kernel(cq_ref, kv_ref, ki_ref, qg_ref, kvg_ref, ig_ref, ib_ref, wuq_ref, wuk_ref,
                wiq_ref, wuv_ref, o_ref, bias_ref, dist_ref, *, q0, n_sel):
    tq = cq_ref.shape[0]
    sk = kv_ref.shape[0]

    cq = cq_ref[...]
    cq = cq * lax.rsqrt(jnp.mean(cq * cq, axis=-1, keepdims=True) + LN_EPS) * qg_ref[...]
    cqb = cq.astype(BF16)
    kv = kv_ref[...]
    ckv = (kv * lax.rsqrt(jnp.mean(kv * kv, axis=-1, keepdims=True) + LN_EPS)
           * kvg_ref[...]).astype(BF16)
    kiw = ki_ref[...]
    kraw = kiw[:, :D_IDX]
    mu = jnp.mean(kraw, axis=-1, keepdims=True)
    kc = kraw - mu
    var = jnp.mean(kc * kc, axis=-1, keepdims=True)
    kidx = (kc * lax.rsqrt(var + LN_EPS) * ig_ref[...] + ib_ref[...]).astype(BF16)
    widx = kiw[q0:q0 + tq, D_IDX:D_IDX + H_IDX] * (H_IDX ** -0.5)

    score = None
    for h in range(H_IDX):
        qi = jnp.dot(cqb, wiq_ref[h], preferred_element_type=F32).astype(BF16)
        r = lax.dot_general(qi, kidx, (((1,), (1,)), ((), ())), preferred_element_type=F32)
        term = jnp.maximum(r * (D_IDX ** -0.5), 0.0) * widx[:, h:h + 1]
        score = term if score is None else score + term
    t_pos = q0 + lax.broadcasted_iota(I32, (tq, sk), 0)
    s_pos = lax.broadcasted_iota(I32, (tq, sk), 1)
    allowed = (s_pos >> CHUNK_SHIFT) <= (t_pos >> CHUNK_SHIFT)
    score = jnp.where(allowed, score, NEG)

    bits = lax.bitcast_convert_type(score, I32)
    key = jnp.where(bits < 0, bits ^ 0x7FFFFFFF, bits)
    key = jnp.where(key == -1, 0, key)

    ones_k = jnp.ones((sk, LANES), BF16)

    def count(mask):
        m = jnp.where(mask, 1.0, 0.0).astype(BF16)
        return jnp.dot(m, ones_k, preferred_element_type=F32)[:, :1]

    nsel = float(n_sel)
    cur = jnp.where(count(key >= 0) >= nsel, 0, INT_MIN).astype(I32)

    def search(i, cur):
        cand = cur | lax.shift_left(jnp.int32(1), 30 - i)
        return jnp.where(count(key >= cand) >= nsel, cand, cur)

    cur = lax.fori_loop(0, 31, search, cur)

    need = nsel - count(key > cur)
    ri = lax.broadcasted_iota(I32, (LANES, LANES), 0)
    ci = lax.broadcasted_iota(I32, (LANES, LANES), 1)
    tri = jnp.where(ri < ci, 1.0, 0.0).astype(BF16)
    ones_b = jnp.ones((LANES, LANES), BF16)
    off = jnp.zeros((tq, 1), F32)
    for c in range(sk // LANES):
        sl = slice(c * LANES, (c + 1) * LANES)
        keyc = key[:, sl]
        eqc = jnp.where(keyc == cur, 1.0, 0.0)
        eqb = eqc.astype(BF16)
        rank = jnp.dot(eqb, tri, preferred_element_type=F32) + off
        sel = jnp.where(keyc > cur, 1.0, jnp.where(rank < need, eqc, 0.0))
        tp = q0 + lax.broadcasted_iota(I32, (tq, LANES), 0)
        sp = c * LANES + lax.broadcasted_iota(I32, (tq, LANES), 1)
        ok = (sp >> CHUNK_SHIFT) <= (tp >> CHUNK_SHIFT)
        bias_ref[:, sl] = jnp.where(ok, jnp.where(sel > 0.5, 0.0, NEG), NEG)
        dist_ref[:, sl] = jnp.abs(tp - sp).astype(F32)
        off = off + jnp.dot(eqb, ones_b, preferred_element_type=F32)[:, :1]

    acc = jnp.zeros((tq, H_A * DH_A), F32)
    for h in range(H_A):
        slope = 2.0 ** (-8.0 * (h + 1) / H_A)
        qh = jnp.dot(cqb, wuq_ref[h], preferred_element_type=F32).astype(BF16)
        ql = (jnp.dot(qh, wuk_ref[h], preferred_element_type=F32)
              * (DH_A ** -0.5)).astype(BF16)
        lg = lax.dot_general(ql, ckv, (((1,), (1,)), ((), ())), preferred_element_type=F32)
        lg = lg - slope * dist_ref[...] + bias_ref[...]
        m = jnp.max(lg, axis=-1, keepdims=True)
        p = jnp.exp(lg - m)
        l = jnp.sum(p, axis=-1, keepdims=True)
        ol = jnp.dot(p.astype(BF16), ckv, preferred_element_type=F32) * (1.0 / l)
        acc = acc + jnp.dot(ol.astype(BF16), wuv_ref[h], preferred_element_type=F32)
    o_ref[...] = acc.astype(o_ref.dtype)


def _dsa_tile(proj3, qt, tq, n_sel, qg, kvg, ig, ib, wuq, wuk, wiq, wuv):
    b, s, _ = proj3.shape
    sk = (qt + 1) * tq
    full = lambda a: pl.BlockSpec(a.shape, lambda i: (0,) * a.ndim)
    return pl.pallas_call(
        functools.partial(_dsa_kernel, q0=qt * tq, n_sel=n_sel),
        grid=(b,),
        in_specs=[pl.BlockSpec((None, tq, D_CQ), lambda i: (i, qt, 0)),
                  pl.BlockSpec((None, sk, D_C), lambda i: (i, 0, D_CQ // D_C)),
                  pl.BlockSpec((None, sk, LANES), lambda i: (i, 0, IDX_COLS // LANES)),
                  full(qg), full(kvg), full(ig), full(ib), full(wuq), full(wuk), full(wiq),
                  full(wuv)],
        out_specs=pl.BlockSpec((None, tq, H_A * DH_A), lambda i: (i, 0, 0)),
        out_shape=jax.ShapeDtypeStruct((b, tq, H_A * DH_A), BF16),
        scratch_shapes=[pltpu.VMEM((tq, sk), F32), pltpu.VMEM((tq, sk), F32)],
        compiler_params=_cparams(("parallel",)),
        name=f"dsa_q{qt}",
    )(proj3, proj3, proj3, qg, kvg, ig, ib, wuq, wuk, wiq, wuv)


def _gelu(x):
    return 0.5 * x * (1.0 + jnp.tanh(0.7978845608028654 * (x + 0.044715 * (x * x * x))))


def _gmlp_kernel(u_ref, v_ref, vg_ref, vb_ref, ws_ref, bias_ref, o_ref):
    tg = u_ref.shape[0]
    ri = lax.broadcasted_iota(I32, (GMLP_CHUNK, GMLP_CHUNK), 0)
    ci = lax.broadcasted_iota(I32, (GMLP_CHUNK, GMLP_CHUNK), 1)
    tril = ri >= ci
    grp = lax.broadcasted_iota(I32, (GMLP_CHUNK, GMLP_WIDTH), 1) // CG_B
    ws = [jnp.where(tril, ws_ref[g], 0.0).astype(BF16) for g in range(G_B)]
    for c in range(tg // GMLP_CHUNK):
        sl = slice(c * GMLP_CHUNK, (c + 1) * GMLP_CHUNK)
        u = _gelu(u_ref[sl, :])
        v = _gelu(v_ref[sl, :])
        mu = jnp.mean(v, axis=-1, keepdims=True)
        vc = v - mu
        var = jnp.mean(vc * vc, axis=-1, keepdims=True)
        vn = (vc * lax.rsqrt(var + LN_EPS) * vg_ref[...] + vb_ref[...]).astype(BF16)
        mixed = bias_ref[...]
        for g in range(G_B):
            r = jnp.dot(ws[g], vn, preferred_element_type=F32)
            mixed = mixed + jnp.where(grp == g, r, 0.0)
        o_ref[sl, :] = (u * mixed).astype(o_ref.dtype)


def _gmlp(proj3, vg, vb, ws, bias):
    b, s, _ = proj3.shape
    tg = min(GMLP_TG, s)
    full = lambda a: pl.BlockSpec(a.shape, lambda i, j: (0,) * a.ndim)
    return pl.pallas_call(
        _gmlp_kernel,
        grid=(b, s // tg),
        in_specs=[pl.BlockSpec((None, tg, GMLP_WIDTH), lambda i, j: (i, j, 1)),
                  pl.BlockSpec((None, tg, GMLP_WIDTH), lambda i, j: (i, j, 2)),
                  full(vg), full(vb), full(ws), full(bias)],
        out_specs=pl.BlockSpec((None, tg, GMLP_WIDTH), lambda i, j: (i, j, 0)),
        out_shape=jax.ShapeDtypeStruct((b, s, GMLP_WIDTH), BF16),
        compiler_params=_cparams(("parallel", "parallel")),
        name="gmlp",
    )(proj3, proj3, vg, vb, ws, bias)


def _diff_kernel(slope_ref, q_ref, k_ref, v_ref, lq1_ref, lk1_ref, lq2_ref, lk2_ref, g_ref,
                 o_ref, nd_ref, *, q0, lam_init):
    tq = q_ref.shape[0]
    sk = k_ref.shape[0]
    h = pl.program_id(1)

    @pl.when((pl.program_id(0) == 0) & (h == 0))
    def _():
        t_pos = q0 + lax.broadcasted_iota(I32, (tq, sk), 0)
        s_pos = lax.broadcasted_iota(I32, (tq, sk), 1)
        ok = (s_pos >> CHUNK_SHIFT) <= (t_pos >> CHUNK_SHIFT)
        nd_ref[...] = jnp.where(ok, -jnp.abs(t_pos - s_pos).astype(F32), NEG)

    lam = (jnp.exp(jnp.sum(lq1_ref[...] * lk1_ref[...], axis=-1, keepdims=True))
           - jnp.exp(jnp.sum(lq2_ref[...] * lk2_ref[...], axis=-1, keepdims=True)) + lam_init)
    slope = slope_ref[h]
    lane = lax.broadcasted_iota(I32, (tq, 2 * DH_C), 1)
    q = q_ref[...] * (DH_C ** -0.5)
    zero = jnp.zeros_like(q)
    k = k_ref[...]
    v = v_ref[...]
    outs = []
    for m in range(2):
        qm = jnp.where((lane < DH_C) if m == 0 else (lane >= DH_C), q, zero)
        lg = lax.dot_general(qm, k, (((1,), (1,)), ((), ())), preferred_element_type=F32)
        lg = lg + slope * nd_ref[...]
        mx = jnp.max(lg, axis=-1, keepdims=True)
        p = jnp.exp(lg - mx)
        l = jnp.sum(p, axis=-1, keepdims=True)
        outs.append(jnp.dot(p.astype(BF16), v, preferred_element_type=F32) * (1.0 / l))
    o = outs[0] - lam * outs[1]
    o = o * lax.rsqrt(jnp.mean(o * o, axis=-1, keepdims=True) + LN_EPS) * g_ref[...]
    o_ref[...] = (o * (1.0 - lam_init)).astype(o_ref.dtype)


def _diff_tile(slopes, proj3, qt, tq, lam_init, lq1, lk1, lq2, lk2, g):
    b, s, _ = proj3.shape
    sk = (qt + 1) * tq
    w = 2 * DH_C
    full = lambda a: pl.BlockSpec(a.shape, lambda i, j: (0,) * a.ndim)
    return pl.pallas_call(
        functools.partial(_diff_kernel, q0=qt * tq, lam_init=lam_init),
        grid=(b, H_C),
        in_specs=[pl.BlockSpec(memory_space=pltpu.SMEM),
                  pl.BlockSpec((None, tq, w), lambda i, j: (i, qt, j)),
                  pl.BlockSpec((None, sk, w), lambda i, j: (i, 0, H_C + j)),
                  pl.BlockSpec((None, sk, w), lambda i, j: (i, 0, 2 * H_C + j)),
                  full(lq1), full(lk1), full(lq2), full(lk2), full(g)],
        out_specs=pl.BlockSpec((None, tq, w), lambda i, j: (i, 0, j)),
        out_shape=jax.ShapeDtypeStruct((b, tq, H_C * w), BF16),
        scratch_shapes=[pltpu.VMEM((tq, sk), F32)],
        compiler_params=_cparams(("arbitrary", "arbitrary")),
        name=f"diff_q{qt}",
    )(slopes, proj3, proj3, proj3, lq1, lk1, lq2, lk2, g)


def _layer_norm_rows(z, g, b):
    mu = jnp.mean(z, axis=-1, keepdims=True)
    zc = z - mu
    var = jnp.mean(zc * zc, axis=-1, keepdims=True)
    return zc * lax.rsqrt(var + LN_EPS) * g + b


def _outln_kernel(*refs, n_in):
    xs = refs[:n_in]
    ws = refs[n_in:2 * n_in]
    h_ref, g_ref, b_ref, rw_ref, rb_ref, h1_ref, te_ref, tg_ref = refs[2 * n_in:]
    acc = None
    for x_ref, w_ref in zip(xs, ws):
        d = jnp.dot(x_ref[...], w_ref[...], preferred_element_type=F32)
        acc = d if acc is None else acc + d
    h1 = _layer_norm_rows(DN_ALPHA * h_ref[...] + acc, g_ref[...], b_ref[...])
    h1_ref[...] = h1

    logits = jnp.dot(h1, rw_ref[...], precision=lax.Precision.HIGHEST,
                     preferred_element_type=F32) + rb_ref[...]
    lane = lax.broadcasted_iota(I32, logits.shape, 1)
    vals, idxs = [], []
    for _ in range(TOP_K):
        mx = jnp.max(logits, axis=-1, keepdims=True)
        ix = jnp.min(jnp.where(logits == mx, lane, LANES), axis=-1, keepdims=True)
        vals.append(mx)
        idxs.append(ix)
        logits = jnp.where(lane == ix, -jnp.inf, logits)
    es = [jnp.exp(v - vals[0]) for v in vals]
    den = es[0] + es[1] + es[2] + es[3]
    te = jnp.zeros(lane.shape, I32)
    tg = jnp.zeros(lane.shape, F32)
    for k in range(TOP_K):
        te = jnp.where(lane == k, idxs[k], te)
        tg = jnp.where(lane == k, es[k] / den, tg)
    te_ref[...] = te
    tg_ref[...] = tg


def _outln(xs, ws, h2, g, b, rw, rb):
    m = h2.shape[0]
    tm = min(PROJ_TM, m)
    n_in = len(xs)
    full = lambda a: pl.BlockSpec(a.shape, lambda i: (0,) * a.ndim)
    row = lambda a: pl.BlockSpec((tm, a.shape[1]), lambda i: (i, 0))
    return pl.pallas_call(
        functools.partial(_outln_kernel, n_in=n_in),
        grid=(m // tm,),
        in_specs=[row(x) for x in xs] + [full(w) for w in ws]
        + [row(h2), full(g), full(b), full(rw), full(rb)],
        out_specs=[pl.BlockSpec((tm, D_MODEL), lambda i: (i, 0)),
                   pl.BlockSpec((tm, LANES), lambda i: (i, 0)),
                   pl.BlockSpec((tm, LANES), lambda i: (i, 0))],
        out_shape=[jax.ShapeDtypeStruct((m, D_MODEL), F32),
                   jax.ShapeDtypeStruct((m, LANES), I32),
                   jax.ShapeDtypeStruct((m, LANES), F32)],
        compiler_params=_cparams(("parallel",)),
        name="outln_router",
    )(*xs, *ws, h2, g, b, rw, rb)


def _row_copy(src_hbm, dst_vmem, sem, src_row, dst_row):
    return pltpu.make_async_copy(src_hbm.at[pl.ds(src_row, 1), :],
                                 dst_vmem.at[pl.ds(dst_row, 1), :], sem)


def _gather_rows(idx_smem, src_hbm, dst_vmem, sem, n):
    def issue(r, c):
        _row_copy(src_hbm, dst_vmem, sem, idx_smem[r], r).start()
        return c

    lax.fori_loop(0, n, issue, 0)

    def drain(r, c):
        _row_copy(src_hbm, dst_vmem, sem, 0, r).wait()
        return c

    lax.fori_loop(0, n, drain, 0)


def _dispatch_kernel(idx_hbm, src_hbm, o_ref, idx_smem, sem_i, sem):
    i = pl.program_id(0)
    cp = pltpu.make_async_copy(idx_hbm.at[i], idx_smem, sem_i)
    cp.start()
    cp.wait()
    _gather_rows(idx_smem, src_hbm, o_ref, sem, o_ref.shape[0])


def _dispatch(row_tok2, h1):
    nb, tm = row_tok2.shape
    d = h1.shape[1]
    return pl.pallas_call(
        _dispatch_kernel,
        grid=(nb,),
        in_specs=[pl.BlockSpec(memory_space=pl.ANY), pl.BlockSpec(memory_space=pl.ANY)],
        out_specs=pl.BlockSpec((tm, d), lambda i: (i, 0)),
        out_shape=jax.ShapeDtypeStruct((nb * tm, d), h1.dtype),
        scratch_shapes=[pltpu.SMEM((tm,), I32), pltpu.SemaphoreType.DMA(()),
                        pltpu.SemaphoreType.DMA(())],
        compiler_params=_cparams(("arbitrary",)),
        name="moe_dispatch",
    )(row_tok2, h1)


def _ffn_kernel(be_ref, nb_ref, x_ref, wgu_ref, bgu_ref, wdn_ref, bdn_ref, y_ref):
    i = pl.program_id(0)

    @pl.when(i < nb_ref[0])
    def _():
        x = x_ref[...].astype(BF16)
        acc = jnp.zeros(y_ref.shape, F32)
        for c in range(D_FF // FFN_FC):
            gs = slice(c * FFN_FC, (c + 1) * FFN_FC)
            us = slice(D_FF + c * FFN_FC, D_FF + (c + 1) * FFN_FC)
            gate = jnp.dot(x, wgu_ref[:, gs], preferred_element_type=F32) + bgu_ref[:, gs]
            up = jnp.dot(x, wgu_ref[:, us], preferred_element_type=F32) + bgu_ref[:, us]
            gate = jnp.minimum(gate, SWIGLU_LIMIT)
            up = jnp.clip(up, -SWIGLU_LIMIT, SWIGLU_LIMIT)
            act = (up + 1.0) * (gate * jax.nn.sigmoid(gate * SWIGLU_ALPHA))
            acc = acc + jnp.dot(act.astype(BF16), wdn_ref[gs, :], preferred_element_type=F32)
        y_ref[...] = acc + bdn_ref[...]

    @pl.when(i >= nb_ref[0])
    def _():
        y_ref[...] = jnp.zeros(y_ref.shape, y_ref.dtype)


def _ffn(block_e, n_used, xs, wgu, bgu, wdn, bdn):
    n_rows, d = xs.shape
    nb = n_rows // FFN_TM
    grid_spec = pltpu.PrefetchScalarGridSpec(
        num_scalar_prefetch=2,
        grid=(nb,),
        in_specs=[pl.BlockSpec((FFN_TM, d), lambda i, be, nu: (i, 0)),
                  pl.BlockSpec((None, d, 2 * D_FF), lambda i, be, nu: (be[i], 0, 0)),
                  pl.BlockSpec((None, 1, 2 * D_FF), lambda i, be, nu: (be[i], 0, 0)),
                  pl.BlockSpec((None, D_FF, d), lambda i, be, nu: (be[i], 0, 0)),
                  pl.BlockSpec((None, 1, d), lambda i, be, nu: (be[i], 0, 0))],
        out_specs=pl.BlockSpec((FFN_TM, d), lambda i, be, nu: (i, 0)),
    )
    return pl.pallas_call(
        _ffn_kernel,
        grid_spec=grid_spec,
        out_shape=jax.ShapeDtypeStruct((n_rows, d), F32),
        compiler_params=_cparams(("arbitrary",)),
        name="moe_ffn",
    )(block_e, n_used, xs, wgu, bgu, wdn, bdn)


def _combine_kernel(pos_hbm, y_hbm, h_ref, gate_ref, g_ref, b_ref, o_ref, idx_smem, ybuf, sem_i,
                    sem):
    i = pl.program_id(0)
    tm = h_ref.shape[0]
    cp = pltpu.make_async_copy(pos_hbm.at[i], idx_smem, sem_i)
    cp.start()
    cp.wait()
    _gather_rows(idx_smem, y_hbm, ybuf, sem, TOP_K * tm)
    gate = gate_ref[...]
    z = DN_ALPHA * h_ref[...]
    for k in range(TOP_K):
        z = z + gate[:, k:k + 1] * ybuf[k * tm:(k + 1) * tm, :]
    o_ref[...] = _layer_norm_rows(z, g_ref[...], b_ref[...])


def _combine(pos_tiles, y, h1, gates, g, b):
    m, d = h1.shape
    nt, w = pos_tiles.shape
    tm = w // TOP_K
    full = lambda a: pl.BlockSpec(a.shape, lambda i: (0,) * a.ndim)
    return pl.pallas_call(
        _combine_kernel,
        grid=(nt,),
        in_specs=[pl.BlockSpec(memory_space=pl.ANY), pl.BlockSpec(memory_space=pl.ANY),
                  pl.BlockSpec((tm, d), lambda i: (i, 0)),
                  pl.BlockSpec((tm, LANES), lambda i: (i, 0)),
                  full(g), full(b)],
        out_specs=pl.BlockSpec((tm, d), lambda i: (i, 0)),
        out_shape=jax.ShapeDtypeStruct((m, d), F32),
        scratch_shapes=[pltpu.SMEM((w,), I32), pltpu.VMEM((w, d), F32),
                        pltpu.SemaphoreType.DMA(()), pltpu.SemaphoreType.DMA(())],
        compiler_params=_cparams(("arbitrary",)),
        name="moe_combine",
    )(pos_tiles, y, h1, gates, g, b)


def _routing_tables(top_e, n_tok):
    n_asg = n_tok * TOP_K
    n_blocks = n_asg // FFN_TM + N_EXPERTS
    onehot = (top_e[:, :, None] == jnp.arange(N_EXPERTS, dtype=I32)[None, None, :]).astype(I32)
    member = jnp.sum(onehot, axis=1)
    csum = jnp.cumsum(member, axis=0)
    rank = csum - member
    counts = csum[-1]
    padded = (counts + FFN_TM - 1) // FFN_TM * FFN_TM
    pad_end = jnp.cumsum(padded)
    pad_start = pad_end - padded
    pos = jnp.sum(onehot * (pad_start[None, :] + rank)[:, None, :], axis=-1)
    tok = jnp.broadcast_to(jnp.arange(n_tok, dtype=I32)[:, None], (n_tok, TOP_K))
    row_tok = jnp.zeros((n_blocks * FFN_TM,), I32).at[pos.reshape(-1)].set(tok.reshape(-1))
    block_e = jnp.minimum(
        jnp.searchsorted(pad_end, jnp.arange(n_blocks, dtype=I32) * FFN_TM, side="right"),
        N_EXPERTS - 1).astype(I32)
    n_used = (pad_end[-1] // FFN_TM).astype(I32).reshape(1)
    return pos.astype(I32), row_tok.reshape(n_blocks, FFN_TM), block_e, n_used


def _moe(h1, top_e, gates, wgu, bgu, wdn, bdn, g, b):
    n_tok = h1.shape[0]
    pos, row_tok2, block_e, n_used = _routing_tables(top_e, n_tok)
    xs = _dispatch(row_tok2, h1)
    y = _ffn(block_e, n_used, xs, wgu, bgu, wdn, bdn)
    tm = min(COMB_TM, n_tok)
    pos_tiles = pos.reshape(n_tok // tm, tm, TOP_K).transpose(0, 2, 1).reshape(n_tok // tm, TOP_K * tm)
    return _combine(pos_tiles, y, h1, gates, g, b)


def _row(a):
    return a.reshape(1, -1).astype(F32)


def _even_layer(h2, bsz, seq, w_in, q_norm, kv_norm, w_uq, w_uk, w_uv, w_iq, idx_g, idx_b, v_g, v_b,
                w_s, b_s, w_out, ln_g, ln_b, rw, rb):
    o4 = D_CQ + D_C + D_IDX + H_IDX
    w_in_p = jnp.concatenate(
        [w_in[:, :o4], jnp.zeros((D_MODEL, E_IN_PAD - E_IN_EVEN), F32), w_in[:, o4:]], axis=1)
    proj3 = _proj(h2, w_in_p.astype(BF16), F32).reshape(bsz, seq, E_IN_PAD)

    n_sel = min(TOPK_MAX, seq // 4)
    tq = min(DSA_TQ, seq)
    wuq = jnp.transpose(w_uq, (1, 0, 2)).astype(BF16)
    wiq = jnp.transpose(w_iq, (1, 0, 2)).astype(BF16)
    wuk = w_uk.astype(BF16)
    eye = jnp.eye(H_A, dtype=F32)
    wuv = (w_uv[:, :, None, :] * eye[:, None, :, None]).reshape(H_A, D_C, H_A * DH_A).astype(BF16)
    o_a = jnp.concatenate(
        [_dsa_tile(proj3, qt, tq, n_sel, _row(q_norm), _row(kv_norm), _row(idx_g), _row(idx_b),
                   wuq, wuk, wiq, wuv) for qt in range(seq // tq)], axis=1)

    gbias = jnp.repeat(jnp.transpose(b_s), CG_B, axis=1)
    o_b = _gmlp(proj3, _row(v_g), _row(v_b), w_s, gbias)

    n_a = H_A * DH_A
    w_out_b = w_out.astype(BF16)
    return _outln([o_a.reshape(bsz * seq, n_a), o_b.reshape(bsz * seq, GMLP_WIDTH)],
                  [w_out_b[:n_a], w_out_b[n_a:]], h2, _row(ln_g), _row(ln_b), rw, rb)


def _odd_layer(h2, bsz, seq, layer, w_in, lq1, lk1, lq2, lk2, subln_g, w_out, ln_g, ln_b, rw, rb):
    lam_init = 0.8 - 0.6 * math.exp(-0.3 * layer)
    proj3 = _proj(h2, w_in.astype(BF16), BF16).reshape(bsz, seq, 3 * QK_W_C)
    tq = min(DIFF_TQ, seq)
    slopes = jnp.exp2(-8.0 * jnp.arange(1, H_C + 1, dtype=F32) / H_C)
    o = jnp.concatenate(
        [_diff_tile(slopes, proj3, qt, tq, lam_init, _row(lq1), _row(lk1), _row(lq2), _row(lk2),
                    _row(subln_g)) for qt in range(seq // tq)], axis=1)
    return _outln([o.reshape(bsz * seq, QK_W_C)], [w_out.astype(BF16)], h2, _row(ln_g), _row(ln_b),
                  rw, rb)


def kernel(x, ev_w_in, ev_q_norm, ev_kv_norm, ev_w_uq, ev_w_uk, ev_w_uv, ev_w_iq, ev_idx_k_g, ev_idx_k_b, ev_v_norm_g, ev_v_norm_b, ev_w_s, ev_b_s, ev_w_out, od_w_in, od_lambda_q1, od_lambda_k1, od_lambda_q2, od_lambda_k2, od_subln_g, od_w_out, ln1_g, ln1_b, ln2_g, ln2_b, router_w, router_b, exp_w_gu, exp_b_gu, exp_w_dn, exp_b_dn):
    bsz, seq, d = x.shape
    h2 = x.reshape(bsz * seq, d)
    for l in range(DEPTH):
        j = l // 2
        rw = jnp.pad(router_w[l], ((0, 0), (0, LANES - N_EXPERTS)))
        rb = jnp.pad(router_b[l], (0, LANES - N_EXPERTS), constant_values=-jnp.inf).reshape(1, LANES)
        if l % 2 == 0:
            h1, te, tg = _even_layer(h2, bsz, seq, ev_w_in[j], ev_q_norm[j], ev_kv_norm[j], ev_w_uq[j],
                                     ev_w_uk[j], ev_w_uv[j], ev_w_iq[j], ev_idx_k_g[j], ev_idx_k_b[j],
                                     ev_v_norm_g[j], ev_v_norm_b[j], ev_w_s[j], ev_b_s[j], ev_w_out[j],
                                     ln1_g[l], ln1_b[l], rw, rb)
        else:
            h1, te, tg = _odd_layer(h2, bsz, seq, l, od_w_in[j], od_lambda_q1[j], od_lambda_k1[j],
                                    od_lambda_q2[j], od_lambda_k2[j], od_subln_g[j], od_w_out[j],
                                    ln1_g[l], ln1_b[l], rw, rb)
        h2 = _moe(h1, te[:, :TOP_K], tg,
                  exp_w_gu[l].astype(BF16), exp_b_gu[l][:, None, :],
                  exp_w_dn[l].astype(BF16), exp_b_dn[l][:, None, :],
                  _row(ln2_g[l]), _row(ln2_b[l]))
    return h2.reshape(bsz, seq, d)
```

```python
import functools
import math

import jax
import jax.numpy as jnp
from jax import lax
from jax.experimental import pallas as pl
from jax.experimental.pallas import tpu as pltpu
from jax.experimental.pallas import tpu_sc as plsc

F32 = jnp.float32
BF16 = jnp.bfloat16
I32 = jnp.int32

D_MODEL = 1024
DEPTH = 4
CHUNK = 64
CHUNK_SHIFT = 6
H_A = 8
DH_A = 64
D_CQ = 256
D_C = 128
H_IDX = 4
D_IDX = 64
TOPK_MAX = 256
GMLP_CHUNK = 128
G_B = 8
GMLP_WIDTH = 512
CG_B = GMLP_WIDTH // G_B
H_C = 8
DH_C = 64
QK_W_C = H_C * 2 * DH_C
N_EXPERTS = 32
TOP_K = 4
D_FF = 1024
SWIGLU_ALPHA = 1.702
SWIGLU_LIMIT = 7.0
DN_ALPHA = (2 * DEPTH) ** 0.25
E_IN_EVEN = D_CQ + D_C + D_IDX + H_IDX + 2 * GMLP_WIDTH
LN_EPS = 1e-5
NEG = -1e30

LANES = 128
E_IN_PAD = 1536
IDX_COLS = D_CQ + D_C
INT_MIN = -(2 ** 31)
HALF = D_MODEL // 2
HI_MASK = -65536
SC_WIN = 64

PROJ_TM = 512
DSA_TQ = 256
DIFF_TQ = 512
GMLP_TG = 512
FFN_TM = 512
FFN_FC = 512
COMB_TM = 512
VMEM_LIMIT = 56 * 1024 * 1024


def _cparams(sem):
    return pltpu.CompilerParams(dimension_semantics=sem, vmem_limit_bytes=VMEM_LIMIT)


def _proj_kernel(x_ref, w_ref, o_ref):
    o_ref[...] = jnp.dot(x_ref[...].astype(BF16), w_ref[...],
                         preferred_element_type=F32).astype(o_ref.dtype)


def _proj(x2, w, out_dtype):
    m, k = x2.shape
    n = w.shape[1]
    tm = min(PROJ_TM, m)
    return pl.pallas_call(
        _proj_kernel,
        grid=(m // tm,),
        in_specs=[pl.BlockSpec((tm, k), lambda i: (i, 0)),
                  pl.BlockSpec((k, n), lambda i: (0, 0))],
        out_specs=pl.BlockSpec((tm, n), lambda i: (i, 0)),
        out_shape=jax.ShapeDtypeStruct((m, n), out_dtype),
        compiler_params=_cparams(("parallel",)),
        name="proj",
    )(x2, w)


def _dsa_kernel(cq_ref, kv_ref, ki_ref, qg_ref, kvg_ref, ig_ref, ib_ref, wuq_ref, wuk_ref,
                wiq_ref, wuv_ref, o_ref, bias_ref, dist_ref, *, q0, n_sel):
    tq = cq_ref.shape[0]
    sk = kv_ref.shape[0]

    cq = cq_ref[...]
    cq = cq * lax.rsqrt(jnp.mean(cq * cq, axis=-1, keepdims=True) + LN_EPS) * qg_ref[...]
    cqb = cq.astype(BF16)
    kv = kv_ref[...]
    ckv = (kv * lax.rsqrt(jnp.mean(kv * kv, axis=-1, keepdims=True) + LN_EPS)
           * kvg_ref[...]).astype(BF16)
    kiw = ki_ref[...]
    kraw = kiw[:, :D_IDX]
    mu = jnp.mean(kraw, axis=-1, keepdims=True)
    kc = kraw - mu
    var = jnp.mean(kc * kc, axis=-1, keepdims=True)
    kidx = (kc * lax.rsqrt(var + LN_EPS) * ig_ref[...] + ib_ref[...]).astype(BF16)
    widx = kiw[q0:q0 + tq, D_IDX:D_IDX + H_IDX] * (H_IDX ** -0.5)

    score = None
    for h in range(H_IDX):
        qi = jnp.dot(cqb, wiq_ref[h], preferred_element_type=F32).astype(BF16)
        r = lax.dot_general(qi, kidx, (((1,), (1,)), ((), ())), preferred_element_type=F32)
        term = jnp.maximum(r * (D_IDX ** -0.5), 0.0) * widx[:, h:h + 1]
        score = term if score is None else score + term
    t_pos = q0 + lax.broadcasted_iota(I32, (tq, sk), 0)
    s_pos = lax.broadcasted_iota(I32, (tq, sk), 1)
    allowed = (s_pos >> CHUNK_SHIFT) <= (t_pos >> CHUNK_SHIFT)
    score = jnp.where(allowed, score, NEG)

    bits = lax.bitcast_convert_type(score, I32)
    key = jnp.where(bits < 0, bits ^ 0x7FFFFFFF, bits)
    key = jnp.where(key == -1, 0, key)

    def count(mask):
        return jnp.sum(jnp.where(mask, 1.0, 0.0), axis=-1, keepdims=True)

    nsel = float(n_sel)
    cur = jnp.where(count(key >= 0) >= nsel, 0, INT_MIN).astype(I32)

    def search(i, cur):
        cand = cur | lax.shift_left(jnp.int32(1), 30 - i)
        return jnp.where(count(key >= cand) >= nsel, cand, cur)

    cur = lax.fori_loop(0, 31, search, cur)

    need = nsel - count(key > cur)
    ri = lax.broadcasted_iota(I32, (LANES, LANES), 0)
    ci = lax.broadcasted_iota(I32, (LANES, LANES), 1)
    tri = jnp.where(ri < ci, 1.0, 0.0).astype(BF16)
    ones_b = jnp.ones((LANES, LANES), BF16)
    off = jnp.zeros((tq, 1), F32)
    for c in range(sk // LANES):
        sl = slice(c * LANES, (c + 1) * LANES)
        keyc = key[:, sl]
        eqc = jnp.where(keyc == cur, 1.0, 0.0)
        eqb = eqc.astype(BF16)
        rank = jnp.dot(eqb, tri, preferred_element_type=F32) + off
        sel = jnp.where(keyc > cur, 1.0, jnp.where(rank < need, eqc, 0.0))
        tp = q0 + lax.broadcasted_iota(I32, (tq, LANES), 0)
        sp = c * LANES + lax.broadcasted_iota(I32, (tq, LANES), 1)
        ok = (sp >> CHUNK_SHIFT) <= (tp >> CHUNK_SHIFT)
        bias_ref[:, sl] = jnp.where(ok, jnp.where(sel > 0.5, 0.0, NEG), NEG)
        dist_ref[:, sl] = jnp.abs(tp - sp).astype(F32)
        off = off + jnp.dot(eqb, ones_b, preferred_element_type=F32)[:, :1]

    acc = jnp.zeros((tq, H_A * DH_A), F32)
    for h in range(H_A):
        slope = 2.0 ** (-8.0 * (h + 1) / H_A)
        qh = jnp.dot(cqb, wuq_ref[h], preferred_element_type=F32).astype(BF16)
        ql = (jnp.dot(qh, wuk_ref[h], preferred_element_type=F32)
              * (DH_A ** -0.5)).astype(BF16)
        lg = lax.dot_general(ql, ckv, (((1,), (1,)), ((), ())), preferred_element_type=F32)
        lg = lg - slope * dist_ref[...] + bias_ref[...]
        m = jnp.max(lg, axis=-1, keepdims=True)
        p = jnp.exp(lg - m)
        l = jnp.sum(p, axis=-1, keepdims=True)
        ol = jnp.dot(p.astype(BF16), ckv, preferred_element_type=F32) * (1.0 / l)
        acc = acc + jnp.dot(ol.astype(BF16), wuv_ref[h], preferred_element_type=F32)
    o_ref[...] = acc.astype(o_ref.dtype)


def _dsa_tile(proj3, qt, tq, n_sel, qg, kvg, ig, ib, wuq, wuk, wiq, wuv):
    b, s, _ = proj3.shape
    sk = (qt + 1) * tq
    full = lambda a: pl.BlockSpec(a.shape, lambda i: (0,) * a.ndim)
    return pl.pallas_call(
        functools.partial(_dsa_kernel, q0=qt * tq, n_sel=n_sel),
        grid=(b,),
        in_specs=[pl.BlockSpec((None, tq, D_CQ), lambda i: (i, qt, 0)),
                  pl.BlockSpec((None, sk, D_C), lambda i: (i, 0, D_CQ // D_C)),
                  pl.BlockSpec((None, sk, LANES), lambda i: (i, 0, IDX_COLS // LANES)),
                  full(qg), full(kvg), full(ig), full(ib), full(wuq), full(wuk), full(wiq),
                  full(wuv)],
        out_specs=pl.BlockSpec((None, tq, H_A * DH_A), lambda i: (i, 0, 0)),
        out_shape=jax.ShapeDtypeStruct((b, tq, H_A * DH_A), BF16),
        scratch_shapes=[pltpu.VMEM((tq, sk), F32), pltpu.VMEM((tq, sk), F32)],
        compiler_params=_cparams(("parallel",)),
        name=f"dsa_q{qt}",
    )(proj3, proj3, proj3, qg, kvg, ig, ib, wuq, wuk, wiq, wuv)


def _gelu(x):
    return 0.5 * x * (1.0 + jnp.tanh(0.7978845608028654 * (x + 0.044715 * (x * x * x))))


def _gmlp_kernel(u_ref, v_ref, vg_ref, vb_ref, ws_ref, bias_ref, o_ref):
    tg = u_ref.shape[0]
    ri = lax.broadcasted_iota(I32, (GMLP_CHUNK, GMLP_CHUNK), 0)
    ci = lax.broadcasted_iota(I32, (GMLP_CHUNK, GMLP_CHUNK), 1)
    tril = ri >= ci
    grp = lax.broadcasted_iota(I32, (GMLP_CHUNK, GMLP_WIDTH), 1) // CG_B
    ws = [jnp.where(tril, ws_ref[g], 0.0).astype(BF16) for g in range(G_B)]
    for c in range(tg // GMLP_CHUNK):
        sl = slice(c * GMLP_CHUNK, (c + 1) * GMLP_CHUNK)
        u = _gelu(u_ref[sl, :])
        v = _gelu(v_ref[sl, :])
        mu = jnp.mean(v, axis=-1, keepdims=True)
        vc = v - mu
        var = jnp.mean(vc * vc, axis=-1, keepdims=True)
        vn = (vc * lax.rsqrt(var + LN_EPS) * vg_ref[...] + vb_ref[...]).astype(BF16)
        mixed = bias_ref[...]
        for g in range(G_B):
            r = jnp.dot(ws[g], vn, preferred_element_type=F32)
            mixed = mixed + jnp.where(grp == g, r, 0.0)
        o_ref[sl, :] = (u * mixed).astype(o_ref.dtype)


def _gmlp(proj3, vg, vb, ws, bias):
    b, s, _ = proj3.shape
    tg = min(GMLP_TG, s)
    full = lambda a: pl.BlockSpec(a.shape, lambda i, j: (0,) * a.ndim)
    return pl.pallas_call(
        _gmlp_kernel,
        grid=(b, s // tg),
        in_specs=[pl.BlockSpec((None, tg, GMLP_WIDTH), lambda i, j: (i, j, 1)),
                  pl.BlockSpec((None, tg, GMLP_WIDTH), lambda i, j: (i, j, 2)),
                  full(vg), full(vb), full(ws), full(bias)],
        out_specs=pl.BlockSpec((None, tg, GMLP_WIDTH), lambda i, j: (i, j, 0)),
        out_shape=jax.ShapeDtypeStruct((b, s, GMLP_WIDTH), BF16),
        compiler_params=_cparams(("parallel", "parallel")),
        name="gmlp",
    )(proj3, proj3, vg, vb, ws, bias)


def _diff_kernel(slope_ref, q_ref, k_ref, v_ref, lq1_ref, lk1_ref, lq2_ref, lk2_ref, g_ref,
                 o_ref, nd_ref, *, q0, lam_init):
    tq = q_ref.shape[0]
    sk = k_ref.shape[0]
    h = pl.program_id(1)

    @pl.when((pl.program_id(0) == 0) & (h == 0))
    def _():
        t_pos = q0 + lax.broadcasted_iota(I32, (tq, sk), 0)
        s_pos = lax.broadcasted_iota(I32, (tq, sk), 1)
        ok = (s_pos >> CHUNK_SHIFT) <= (t_pos >> CHUNK_SHIFT)
        nd_ref[...] = jnp.where(ok, -jnp.abs(t_pos - s_pos).astype(F32), NEG)

    lam = (jnp.exp(jnp.sum(lq1_ref[...] * lk1_ref[...], axis=-1, keepdims=True))
           - jnp.exp(jnp.sum(lq2_ref[...] * lk2_ref[...], axis=-1, keepdims=True)) + lam_init)
    slope = slope_ref[h]
    lane = lax.broadcasted_iota(I32, (tq, 2 * DH_C), 1)
    q = q_ref[...] * (DH_C ** -0.5)
    zero = jnp.zeros_like(q)
    k = k_ref[...]
    v = v_ref[...]
    outs = []
    for m in range(2):
        qm = jnp.where((lane < DH_C) if m == 0 else (lane >= DH_C), q, zero)
        lg = lax.dot_general(qm, k, (((1,), (1,)), ((), ())), preferred_element_type=F32)
        lg = lg + slope * nd_ref[...]
        mx = jnp.max(lg, axis=-1, keepdims=True)
        p = jnp.exp(lg - mx)
        l = jnp.sum(p, axis=-1, keepdims=True)
        outs.append(jnp.dot(p.astype(BF16), v, preferred_element_type=F32) * (1.0 / l))
    o = outs[0] - lam * outs[1]
    o = o * lax.rsqrt(jnp.mean(o * o, axis=-1, keepdims=True) + LN_EPS) * g_ref[...]
    o_ref[...] = (o * (1.0 - lam_init)).astype(o_ref.dtype)


def _diff_tile(slopes, proj3, qt, tq, lam_init, lq1, lk1, lq2, lk2, g):
    b, s, _ = proj3.shape
    sk = (qt + 1) * tq
    w = 2 * DH_C
    full = lambda a: pl.BlockSpec(a.shape, lambda i, j: (0,) * a.ndim)
    return pl.pallas_call(
        functools.partial(_diff_kernel, q0=qt * tq, lam_init=lam_init),
        grid=(b, H_C),
        in_specs=[pl.BlockSpec(memory_space=pltpu.SMEM),
                  pl.BlockSpec((None, tq, w), lambda i, j: (i, qt, j)),
                  pl.BlockSpec((None, sk, w), lambda i, j: (i, 0, H_C + j)),
                  pl.BlockSpec((None, sk, w), lambda i, j: (i, 0, 2 * H_C + j)),
                  full(lq1), full(lk1), full(lq2), full(lk2), full(g)],
        out_specs=pl.BlockSpec((None, tq, w), lambda i, j: (i, 0, j)),
        out_shape=jax.ShapeDtypeStruct((b, tq, H_C * w), BF16),
        scratch_shapes=[pltpu.VMEM((tq, sk), F32)],
        compiler_params=_cparams(("arbitrary", "arbitrary")),
        name=f"diff_q{qt}",
    )(slopes, proj3, proj3, proj3, lq1, lk1, lq2, lk2, g)


def _pack_halves(x):
    lo = lax.bitcast_convert_type(x[:, :HALF].astype(BF16).astype(F32), I32)
    hi = lax.bitcast_convert_type(x[:, HALF:].astype(BF16).astype(F32), I32)
    return lax.shift_right_logical(lo, 16) | (hi & HI_MASK)


def _unpack_lo(w):
    return lax.bitcast_convert_type(lax.shift_left(w, 16), F32)


def _unpack_hi(w):
    return lax.bitcast_convert_type(w & HI_MASK, F32)


def _layer_norm_rows(z, g, b):
    mu = jnp.mean(z, axis=-1, keepdims=True)
    zc = z - mu
    var = jnp.mean(zc * zc, axis=-1, keepdims=True)
    return zc * lax.rsqrt(var + LN_EPS) * g + b


def _outln_kernel(*refs, n_in):
    xs = refs[:n_in]
    ws = refs[n_in:2 * n_in]
    h_ref, g_ref, b_ref, rw_ref, rb_ref, h1_ref, hp_ref, te_ref, tg_ref = refs[2 * n_in:]
    acc = None
    for x_ref, w_ref in zip(xs, ws):
        d = jnp.dot(x_ref[...], w_ref[...], preferred_element_type=F32)
        acc = d if acc is None else acc + d
    h1 = _layer_norm_rows(DN_ALPHA * h_ref[...] + acc, g_ref[...], b_ref[...])
    h1_ref[...] = h1
    hp_ref[...] = _pack_halves(h1)

    logits = jnp.dot(h1, rw_ref[...], precision=lax.Precision.HIGHEST,
                     preferred_element_type=F32) + rb_ref[...]
    lane = lax.broadcasted_iota(I32, logits.shape, 1)
    vals, idxs = [], []
    for _ in range(TOP_K):
        mx = jnp.max(logits, axis=-1, keepdims=True)
        ix = jnp.min(jnp.where(logits == mx, lane, LANES), axis=-1, keepdims=True)
        vals.append(mx)
        idxs.append(ix)
        logits = jnp.where(lane == ix, -jnp.inf, logits)
    es = [jnp.exp(v - vals[0]) for v in vals]
    den = es[0] + es[1] + es[2] + es[3]
    te = jnp.zeros(lane.shape, I32)
    tg = jnp.zeros(lane.shape, F32)
    for k in range(TOP_K):
        te = jnp.where(lane == k, idxs[k], te)
        tg = jnp.where(lane == k, es[k] / den, tg)
    te_ref[...] = te
    tg_ref[...] = tg


def _outln(xs, ws, h2, g, b, rw, rb):
    m = h2.shape[0]
    tm = min(PROJ_TM, m)
    n_in = len(xs)
    full = lambda a: pl.BlockSpec(a.shape, lambda i: (0,) * a.ndim)
    row = lambda a: pl.BlockSpec((tm, a.shape[1]), lambda i: (i, 0))
    return pl.pallas_call(
        functools.partial(_outln_kernel, n_in=n_in),
        grid=(m // tm,),
        in_specs=[row(x) for x in xs] + [full(w) for w in ws]
        + [row(h2), full(g), full(b), full(rw), full(rb)],
        out_specs=[pl.BlockSpec((tm, D_MODEL), lambda i: (i, 0)),
                   pl.BlockSpec((tm, HALF), lambda i: (i, 0)),
                   pl.BlockSpec((tm, LANES), lambda i: (i, 0)),
                   pl.BlockSpec((tm, LANES), lambda i: (i, 0))],
        out_shape=[jax.ShapeDtypeStruct((m, D_MODEL), F32),
                   jax.ShapeDtypeStruct((m, HALF), I32),
                   jax.ShapeDtypeStruct((m, LANES), I32),
                   jax.ShapeDtypeStruct((m, LANES), F32)],
        compiler_params=_cparams(("parallel",)),
        name="outln_router",
    )(*xs, *ws, h2, g, b, rw, rb)


def _sc_mesh():
    return plsc.VectorSubcoreMesh(core_axis_name="core", subcore_axis_name="subcore")


def _pad_windows(idx2):
    return jnp.pad(idx2, ((0, 0), (0, LANES - SC_WIN)))


def _sc_scatter(x, pos):
    t, d = x.shape
    nw = t // SC_WIN
    idx = _pad_windows(pos.reshape(nw, SC_WIN, TOP_K).transpose(0, 2, 1).reshape(nw * TOP_K, SC_WIN))

    @pl.kernel(out_type=jax.ShapeDtypeStruct((t * TOP_K, d), x.dtype), mesh=_sc_mesh())
    def scatter_rows(x_hbm, i_hbm, o_hbm):
        def body(x_vmem, i_vmem):
            pltpu.sync_copy(x_vmem, o_hbm.at[i_vmem.at[0, pl.ds(0, SC_WIN)]])

        pltpu.emit_pipeline(
            body,
            grid=(nw * TOP_K,),
            in_specs=[pl.BlockSpec((SC_WIN, d), lambda i: (i // TOP_K, 0)),
                      pl.BlockSpec((1, LANES), lambda i: (i, 0))],
            out_specs=[],
            core_axis_name=("core", "subcore"),
            dimension_semantics=(pltpu.PARALLEL,),
        )(x_hbm, i_hbm)

    return scatter_rows(x, idx)


def _sc_gather(y, idx):
    n = idx.shape[0]
    d = y.shape[1]
    idx2 = _pad_windows(idx.reshape(n // SC_WIN, SC_WIN))

    @pl.kernel(out_type=jax.ShapeDtypeStruct((n, d), y.dtype), mesh=_sc_mesh())
    def gather_rows(y_hbm, i_hbm, o_hbm):
        def body(i_vmem, o_vmem):
            pltpu.sync_copy(y_hbm.at[i_vmem.at[0, pl.ds(0, SC_WIN)]], o_vmem)

        pltpu.emit_pipeline(
            body,
            grid=(n // SC_WIN,),
            in_specs=[pl.BlockSpec((1, LANES), lambda i: (i, 0))],
            out_specs=[pl.BlockSpec((SC_WIN, d), lambda i: (i, 0))],
            core_axis_name=("core", "subcore"),
            dimension_semantics=(pltpu.PARALLEL,),
        )(i_hbm, o_hbm)

    return gather_rows(y, idx2)


def _ffn_kernel(vb_ref, ve_ref, lo_ref, hi_ref, first_ref, x_ref, wgu_ref, bgu_ref, wdn_ref, bdn_ref,
                y_ref):
    v = pl.program_id(0)
    lo = lo_ref[v]
    hi = hi_ref[v]

    @pl.when(hi > lo)
    def _():
        xw = x_ref[...]
        xl = _unpack_lo(xw).astype(BF16)
        xh = _unpack_hi(xw).astype(BF16)
        acc = jnp.zeros((x_ref.shape[0], D_MODEL), F32)
        for c in range(D_FF // FFN_FC):
            gs = slice(c * FFN_FC, (c + 1) * FFN_FC)
            us = slice(D_FF + c * FFN_FC, D_FF + (c + 1) * FFN_FC)
            gate = (jnp.dot(xl, wgu_ref[:HALF, gs], preferred_element_type=F32)
                    + jnp.dot(xh, wgu_ref[HALF:, gs], preferred_element_type=F32) + bgu_ref[:, gs])
            up = (jnp.dot(xl, wgu_ref[:HALF, us], preferred_element_type=F32)
                  + jnp.dot(xh, wgu_ref[HALF:, us], preferred_element_type=F32) + bgu_ref[:, us])
            gate = jnp.minimum(gate, SWIGLU_LIMIT)
            up = jnp.clip(up, -SWIGLU_LIMIT, SWIGLU_LIMIT)
            act = (up + 1.0) * (gate * jax.nn.sigmoid(gate * SWIGLU_ALPHA))
            acc = acc + jnp.dot(act.astype(BF16), wdn_ref[gs, :], preferred_element_type=F32)
        new = _pack_halves(acc + bdn_ref[...])
        row = lax.broadcasted_iota(I32, new.shape, 0)
        keep = jnp.where(first_ref[v] == 1, jnp.zeros_like(new), y_ref[...])
        y_ref[...] = jnp.where(row >= lo, jnp.where(row < hi, new, keep), keep)


def _ffn(visits, xs, wgu, bgu, wdn, bdn):
    n_rows, hw = xs.shape
    n_vis = visits[0].shape[0]
    blk = lambda v, vb, ve, lo, hi, fi: (vb[v], 0)
    exp = lambda v, vb, ve, lo, hi, fi: (ve[v], 0, 0)
    grid_spec = pltpu.PrefetchScalarGridSpec(
        num_scalar_prefetch=5,
        grid=(n_vis,),
        in_specs=[pl.BlockSpec((FFN_TM, hw), blk),
                  pl.BlockSpec((None, D_MODEL, 2 * D_FF), exp),
                  pl.BlockSpec((None, 1, 2 * D_FF), exp),
                  pl.BlockSpec((None, D_FF, D_MODEL), exp),
                  pl.BlockSpec((None, 1, D_MODEL), exp)],
        out_specs=pl.BlockSpec((FFN_TM, hw), blk),
    )
    return pl.pallas_call(
        _ffn_kernel,
        grid_spec=grid_spec,
        out_shape=jax.ShapeDtypeStruct((n_rows, hw), I32),
        compiler_params=_cparams(("arbitrary",)),
        name="moe_ffn",
    )(*visits, xs, wgu, bgu, wdn, bdn)


def _combine_kernel(yg_ref, h_ref, gate_ref, g_ref, b_ref, o_ref):
    gate = gate_ref[...]
    h = h_ref[...]
    zl = DN_ALPHA * h[:, :HALF]
    zh = DN_ALPHA * h[:, HALF:]
    for k in range(TOP_K):
        w = yg_ref[:, k * HALF:(k + 1) * HALF]
        gk = gate[:, k:k + 1]
        zl = zl + gk * _unpack_lo(w)
        zh = zh + gk * _unpack_hi(w)
    mu = (jnp.sum(zl, axis=-1, keepdims=True) + jnp.sum(zh, axis=-1, keepdims=True)) * (1.0 / D_MODEL)
    zl = zl - mu
    zh = zh - mu
    var = (jnp.sum(zl * zl, axis=-1, keepdims=True)
           + jnp.sum(zh * zh, axis=-1, keepdims=True)) * (1.0 / D_MODEL)
    r = lax.rsqrt(var + LN_EPS)
    o_ref[:, :HALF] = zl * r * g_ref[:, :HALF] + b_ref[:, :HALF]
    o_ref[:, HALF:] = zh * r * g_ref[:, HALF:] + b_ref[:, HALF:]


def _combine(yg, h1, gates, g, b):
    m, d = h1.shape
    tm = min(COMB_TM, m)
    full = lambda a: pl.BlockSpec(a.shape, lambda i: (0,) * a.ndim)
    return pl.pallas_call(
        _combine_kernel,
        grid=(m // tm,),
        in_specs=[pl.BlockSpec((tm, TOP_K * HALF), lambda i: (i, 0)),
                  pl.BlockSpec((tm, d), lambda i: (i, 0)),
                  pl.BlockSpec((tm, LANES), lambda i: (i, 0)),
                  full(g), full(b)],
        out_specs=pl.BlockSpec((tm, d), lambda i: (i, 0)),
        out_shape=jax.ShapeDtypeStruct((m, d), F32),
        compiler_params=_cparams(("parallel",)),
        name="moe_combine",
    )(yg, h1, gates, g, b)


def _routing_tables(top_e, n_tok):
    n_blocks = n_tok * TOP_K // FFN_TM
    n_vis = n_blocks + N_EXPERTS - 1
    onehot = (top_e[:, :, None] == jnp.arange(N_EXPERTS, dtype=I32)[None, None, :]).astype(I32)
    member = jnp.sum(onehot, axis=1)
    csum = jnp.cumsum(member, axis=0)
    rank = csum - member
    counts = csum[-1]
    end = jnp.cumsum(counts)
    start = end - counts
    pos = jnp.sum(onehot * (start[None, :] + rank)[:, None, :], axis=-1).astype(I32)

    first = start // FFN_TM
    last = jnp.maximum(end - 1, 0) // FFN_TM
    nvis = jnp.where(counts > 0, last - first + 1, 0)
    vend = jnp.cumsum(nvis)
    vstart = vend - nvis
    total = vend[-1]
    v = jnp.arange(n_vis, dtype=I32)
    valid = v < total
    ev = jnp.minimum(jnp.searchsorted(vend, v, side="right"), N_EXPERTS - 1).astype(I32)
    ev = jnp.where(valid, ev, ev[jnp.maximum(total - 1, 0)])
    bv = jnp.where(valid, first[ev] + v - vstart[ev], n_blocks - 1).astype(I32)
    lo = jnp.where(valid, jnp.maximum(start[ev], bv * FFN_TM) - bv * FFN_TM, 0).astype(I32)
    hi = jnp.where(valid, jnp.minimum(end[ev], (bv + 1) * FFN_TM) - bv * FFN_TM, 0).astype(I32)
    prev_b = jnp.concatenate([jnp.full((1,), -1, I32), bv[:-1]])
    fi = jnp.where(valid & (bv != prev_b), 1, 0).astype(I32)
    return pos, (bv, ev, lo, hi, fi)


def _moe(h1, h1p, top_e, gates, wgu, bgu, wdn, bdn, g, b):
    n_tok = h1.shape[0]
    pos, visits = _routing_tables(top_e, n_tok)
    xs = _sc_scatter(h1p, pos)
    y = _ffn(visits, xs, wgu, bgu, wdn, bdn)
    yg = _sc_gather(y, pos.reshape(-1)).reshape(n_tok, TOP_K * HALF)
    return _combine(yg, h1, gates, g, b)


def _row(a):
    return a.reshape(1, -1).astype(F32)


def _even_layer(h2, bsz, seq, w_in, q_norm, kv_norm, w_uq, w_uk, w_uv, w_iq, idx_g, idx_b, v_g, v_b,
                w_s, b_s, w_out, ln_g, ln_b, rw, rb):
    o4 = D_CQ + D_C + D_IDX + H_IDX
    w_in_p = jnp.concatenate(
        [w_in[:, :o4], jnp.zeros((D_MODEL, E_IN_PAD - E_IN_EVEN), F32), w_in[:, o4:]], axis=1)
    proj3 = _proj(h2, w_in_p.astype(BF16), F32).reshape(bsz, seq, E_IN_PAD)

    n_sel = min(TOPK_MAX, seq // 4)
    tq = min(DSA_TQ, seq)
    wuq = jnp.transpose(w_uq, (1, 0, 2)).astype(BF16)
    wiq = jnp.transpose(w_iq, (1, 0, 2)).astype(BF16)
    wuk = w_uk.astype(BF16)
    eye = jnp.eye(H_A, dtype=F32)
    wuv = (w_uv[:, :, None, :] * eye[:, None, :, None]).reshape(H_A, D_C, H_A * DH_A).astype(BF16)
    o_a = jnp.concatenate(
        [_dsa_tile(proj3, qt, tq, n_sel, _row(q_norm), _row(kv_norm), _row(idx_g), _row(idx_b),
                   wuq, wuk, wiq, wuv) for qt in range(seq // tq)], axis=1)

    gbias = jnp.repeat(jnp.transpose(b_s), CG_B, axis=1)
    o_b = _gmlp(proj3, _row(v_g), _row(v_b), w_s, gbias)

    n_a = H_A * DH_A
    w_out_b = w_out.astype(BF16)
    return _outln([o_a.reshape(bsz * seq, n_a), o_b.reshape(bsz * seq, GMLP_WIDTH)],
                  [w_out_b[:n_a], w_out_b[n_a:]], h2, _row(ln_g), _row(ln_b), rw, rb)


def _odd_layer(h2, bsz, seq, layer, w_in, lq1, lk1, lq2, lk2, subln_g, w_out, ln_g, ln_b, rw, rb):
    lam_init = 0.8 - 0.6 * math.exp(-0.3 * layer)
    proj3 = _proj(h2, w_in.astype(BF16), BF16).reshape(bsz, seq, 3 * QK_W_C)
    tq = min(DIFF_TQ, seq)
    slopes = jnp.exp2(-8.0 * jnp.arange(1, H_C + 1, dtype=F32) / H_C)
    o = jnp.concatenate(
        [_diff_tile(slopes, proj3, qt, tq, lam_init, _row(lq1), _row(lk1), _row(lq2), _row(lk2),
                    _row(subln_g)) for qt in range(seq // tq)], axis=1)
    return _outln([o.reshape(bsz * seq, QK_W_C)], [w_out.astype(BF16)], h2, _row(ln_g), _row(ln_b),
                  rw, rb)


def kernel(x, ev_w_in, ev_q_norm, ev_kv_norm, ev_w_uq, ev_w_uk, ev_w_uv, ev_w_iq, ev_idx_k_g, ev_idx_k_b, ev_v_norm_g, ev_v_norm_b, ev_w_s, ev_b_s, ev_w_out, od_w_in, od_lambda_q1, od_lambda_k1, od_lambda_q2, od_lambda_k2, od_subln_g, od_w_out, ln1_g, ln1_b, ln2_g, ln2_b, router_w, router_b, exp_w_gu, exp_b_gu, exp_w_dn, exp_b_dn):
    bsz, seq, d = x.shape
    h2 = x.reshape(bsz * seq, d)
    for l in range(DEPTH):
        j = l // 2
        rw = jnp.pad(router_w[l], ((0, 0), (0, LANES - N_EXPERTS)))
        rb = jnp.pad(router_b[l], (0, LANES - N_EXPERTS), constant_values=-jnp.inf).reshape(1, LANES)
        if l % 2 == 0:
            h1, h1p, te, tg = _even_layer(h2, bsz, seq, ev_w_in[j], ev_q_norm[j], ev_kv_norm[j], ev_w_uq[j],
                                     ev_w_uk[j], ev_w_uv[j], ev_w_iq[j], ev_idx_k_g[j], ev_idx_k_b[j],
                                     ev_v_norm_g[j], ev_v_norm_b[j], ev_w_s[j], ev_b_s[j], ev_w_out[j],
                                     ln1_g[l], ln1_b[l], rw, rb)
        else:
            h1, h1p, te, tg = _odd_layer(h2, bsz, seq, l, od_w_in[j], od_lambda_q1[j], od_lambda_k1[j],
                                    od_lambda_q2[j], od_lambda_k2[j], od_subln_g[j], od_w_out[j],
                                    ln1_g[l], ln1_b[l], rw, rb)
        h2 = _moe(h1, h1p, te[:, :TOP_K], tg,
                  exp_w_gu[l].astype(BF16), exp_b_gu[l][:, None, :],
                  exp_w_dn[l].astype(BF16), exp_b_dn[l][:, None, :],
                  _row(ln2_g[l]), _row(ln2_b[l]))
    return h2.reshape(bsz, seq, d)
```

```python
import functools
import math

import jax
import jax.numpy as jnp
from jax import lax
from jax.experimental import pallas as pl
from jax.experimental.pallas import tpu as pltpu
from jax.experimental.pallas import tpu_sc as plsc

F32 = jnp.float32
BF16 = jnp.bfloat16
I32 = jnp.int32

D_MODEL = 1024
DEPTH = 4
CHUNK = 64
CHUNK_SHIFT = 6
H_A = 8
DH_A = 64
D_CQ = 256
D_C = 128
H_IDX = 4
D_IDX = 64
TOPK_MAX = 256
GMLP_CHUNK = 128
G_B = 8
GMLP_WIDTH = 512
CG_B = GMLP_WIDTH // G_B
H_C = 8
DH_C = 64
QK_W_C = H_C * 2 * DH_C
N_EXPERTS = 32
TOP_K = 4
D_FF = 1024
SWIGLU_ALPHA = 1.702
SWIGLU_LIMIT = 7.0
DN_ALPHA = (2 * DEPTH) ** 0.25
E_IN_EVEN = D_CQ + D_C + D_IDX + H_IDX + 2 * GMLP_WIDTH
LN_EPS = 1e-5
NEG = -1e30
LOG2E = 1.4426950408889634

LANES = 128
E_IN_PAD = 1536
INT_MIN = -(2 ** 31)
HALF = D_MODEL // 2
HI_MASK = -65536
SC_WIN = 64
POS_RADIX = 256
N_POS_COLS = 6

PROJ_TM = 512
DSA_TQ = 256
DIFF_TQ = 512
GMLP_TG = 512
FFN_TM = 512
FFN_FC = 512
COMB_TM = 512
VMEM_LIMIT = 56 * 1024 * 1024


def _cparams(sem):
    return pltpu.CompilerParams(dimension_semantics=sem, vmem_limit_bytes=VMEM_LIMIT)


def _full(a, n_grid):
    zeros = (0,) * a.ndim
    return pl.BlockSpec(a.shape, lambda *_: zeros)


def _alibi_slopes_l2(n):
    return jnp.exp2(-8.0 * jnp.arange(1, n + 1, dtype=F32) / n) * LOG2E


def _alibi_query_cols(n):
    s = _alibi_slopes_l2(n)
    hi = s.astype(BF16).astype(F32)
    mid = (s - hi).astype(BF16).astype(F32)
    lo = (s - hi - mid).astype(BF16).astype(F32)
    cols = jnp.stack([hi * POS_RADIX, hi, mid * POS_RADIX, mid, lo * POS_RADIX, lo], axis=-1)
    return jnp.pad(cols, ((0, 0), (0, LANES - N_POS_COLS))).reshape(n, 1, LANES)


def _alibi_key_cols(seq, lane0):
    s = jnp.arange(seq, dtype=I32)
    a = (s // POS_RADIX).astype(F32)
    b = (s % POS_RADIX).astype(F32)
    cols = jnp.stack([a, b, a, b, a, b], axis=-1)
    return jnp.pad(cols, ((0, 0), (lane0, LANES - N_POS_COLS - lane0))).astype(BF16)


def _alibi_diag(n, tq):
    t = jnp.arange(tq, dtype=I32)[:, None]
    s = jnp.arange(tq, dtype=I32)[None, :]
    ok = (s >> CHUNK_SHIFT) <= (t >> CHUNK_SHIFT)
    ahead = jnp.maximum(s - t, 0).astype(F32)
    corr = -2.0 * _alibi_slopes_l2(n)[:, None, None] * ahead[None]
    return jnp.where(ok[None], corr, NEG)


def _proj_even_kernel(x_ref, w_ref, qg_ref, kvg_ref, ig_ref, ib_ref,
                      cq_ref, ckv_ref, kidx_ref, widx_ref, uv_ref):
    acc = jnp.dot(x_ref[...].astype(BF16), w_ref[...], preferred_element_type=F32)
    cq = acc[:, :D_CQ]
    cq_ref[...] = (cq * lax.rsqrt(jnp.mean(cq * cq, axis=-1, keepdims=True) + LN_EPS)
                   * qg_ref[...]).astype(cq_ref.dtype)
    kv = acc[:, D_CQ:D_CQ + D_C]
    ckv_ref[...] = (kv * lax.rsqrt(jnp.mean(kv * kv, axis=-1, keepdims=True) + LN_EPS)
                    * kvg_ref[...]).astype(ckv_ref.dtype)
    blk = acc[:, D_CQ + D_C:D_CQ + D_C + LANES]
    lane = lax.broadcasted_iota(I32, blk.shape, 1)
    is_k = lane < D_IDX
    mu = jnp.sum(jnp.where(is_k, blk, 0.0), axis=-1, keepdims=True) * (1.0 / D_IDX)
    kc = jnp.where(is_k, blk - mu, 0.0)
    var = jnp.sum(kc * kc, axis=-1, keepdims=True) * (1.0 / D_IDX)
    kidx_ref[...] = jnp.where(is_k, kc * lax.rsqrt(var + LN_EPS) * ig_ref[...] + ib_ref[...],
                              0.0).astype(kidx_ref.dtype)
    widx_ref[...] = jnp.where(is_k, 0.0, blk * (H_IDX ** -0.5))
    uv_ref[...] = acc[:, E_IN_PAD - 2 * GMLP_WIDTH:]


def _proj_even(x2, w, qg, kvg, ig, ib):
    m, k = x2.shape
    tm = min(PROJ_TM, m)
    row = lambda n: pl.BlockSpec((tm, n), lambda i: (i, 0))
    consts = (qg, kvg, ig, ib)
    return pl.pallas_call(
        _proj_even_kernel,
        grid=(m // tm,),
        in_specs=[row(k), _full(w, 1)] + [_full(c, 1) for c in consts],
        out_specs=[row(D_CQ), row(D_C), row(LANES), row(LANES), row(2 * GMLP_WIDTH)],
        out_shape=[jax.ShapeDtypeStruct((m, D_CQ), BF16),
                   jax.ShapeDtypeStruct((m, D_C), BF16),
                   jax.ShapeDtypeStruct((m, LANES), BF16),
                   jax.ShapeDtypeStruct((m, LANES), F32),
                   jax.ShapeDtypeStruct((m, 2 * GMLP_WIDTH), F32)],
        compiler_params=_cparams(("parallel",)),
        name="proj_even",
    )(x2, w, *consts)


def _proj_odd_kernel(x_ref, w_ref, o_ref):
    acc = jnp.dot(x_ref[...].astype(BF16), w_ref[...], preferred_element_type=F32)
    o_ref[:, :QK_W_C] = (acc[:, :QK_W_C] * (DH_C ** -0.5 * LOG2E)).astype(o_ref.dtype)
    o_ref[:, QK_W_C:] = acc[:, QK_W_C:].astype(o_ref.dtype)


def _proj_odd(x2, w):
    m, k = x2.shape
    n = w.shape[1]
    tm = min(PROJ_TM, m)
    return pl.pallas_call(
        _proj_odd_kernel,
        grid=(m // tm,),
        in_specs=[pl.BlockSpec((tm, k), lambda i: (i, 0)), _full(w, 1)],
        out_specs=pl.BlockSpec((tm, n), lambda i: (i, 0)),
        out_shape=jax.ShapeDtypeStruct((m, n), BF16),
        compiler_params=_cparams(("parallel",)),
        name="proj_odd",
    )(x2, w)


def _softmax_pv(parts, kv_parts):
    m = None
    for lg in parts:
        mx = jnp.max(lg, axis=-1, keepdims=True)
        m = mx if m is None else jnp.maximum(m, mx)
    l = None
    o = None
    for lg, kv in zip(parts, kv_parts):
        p = jnp.exp2(lg - m)
        s = jnp.sum(p, axis=-1, keepdims=True)
        d = jnp.dot(p.astype(BF16), kv, preferred_element_type=F32)
        l = s if l is None else l + s
        o = d if o is None else o + d
    return o * (1.0 / l)


def _dsa_kernel(cq_ref, ckv_ref, kidx_ref, widx_ref, wuq_ref, wuk_ref, wiq_ref, wuv_ref, srow_ref,
                diag_ref, o_ref, bias_ref, *, q0, n_sel):
    tq = cq_ref.shape[0]
    sk = ckv_ref.shape[0]
    cqb = cq_ref[...]
    kidx = kidx_ref[...]
    widx = widx_ref[...]

    score = None
    for h in range(H_IDX):
        qi = jnp.dot(cqb, wiq_ref[h], preferred_element_type=F32).astype(BF16)
        r = lax.dot_general(qi, kidx, (((1,), (1,)), ((), ())), preferred_element_type=F32)
        term = jnp.maximum(r * (D_IDX ** -0.5), 0.0) * widx[:, D_IDX + h:D_IDX + h + 1]
        score = term if score is None else score + term
    t_pos = q0 + lax.broadcasted_iota(I32, (tq, sk), 0)
    s_pos = lax.broadcasted_iota(I32, (tq, sk), 1)
    allowed = (s_pos >> CHUNK_SHIFT) <= (t_pos >> CHUNK_SHIFT)
    score = jnp.where(allowed, score, NEG)

    bits = lax.bitcast_convert_type(score, I32)
    key = jnp.where(bits < 0, bits ^ 0x7FFFFFFF, bits)
    key = jnp.where(key == -1, 0, key)

    def count(mask):
        return jnp.sum(jnp.where(mask, 1.0, 0.0), axis=-1, keepdims=True)

    nsel = float(n_sel)
    cur = jnp.where(count(key >= 0) >= nsel, 0, INT_MIN).astype(I32)

    def search(i, cur):
        cand = cur | lax.shift_left(jnp.int32(1), 30 - i)
        return jnp.where(count(key >= cand) >= nsel, cand, cur)

    cur = lax.fori_loop(0, 31, search, cur)

    need = nsel - count(key > cur)
    ri = lax.broadcasted_iota(I32, (LANES, LANES), 0)
    ci = lax.broadcasted_iota(I32, (LANES, LANES), 1)
    tri = jnp.where(ri < ci, 1.0, 0.0).astype(BF16)
    ones_b = jnp.ones((LANES, LANES), BF16)
    off = jnp.zeros((tq, 1), F32)
    for c in range(sk // LANES):
        sl = slice(c * LANES, (c + 1) * LANES)
        keyc = key[:, sl]
        eqc = jnp.where(keyc == cur, 1.0, 0.0)
        eqb = eqc.astype(BF16)
        rank = jnp.dot(eqb, tri, preferred_element_type=F32) + off
        sel = jnp.where(keyc > cur, 1.0, jnp.where(rank < need, eqc, 0.0))
        bias_ref[:, sl] = jnp.where(sel > 0.5, 0.0, NEG)
        off = off + jnp.dot(eqb, ones_b, preferred_element_type=F32)[:, :1]

    n_lo = sk - tq
    kv_parts = ([ckv_ref[:n_lo, :]] if n_lo else []) + [ckv_ref[n_lo:, :]]
    acc = jnp.zeros((tq, H_A * DH_A), F32)
    for h in range(H_A):
        qh = jnp.dot(cqb, wuq_ref[h], preferred_element_type=F32).astype(BF16)
        ql = (jnp.dot(qh, wuk_ref[h], preferred_element_type=F32)
              * (DH_A ** -0.5 * LOG2E)).astype(BF16)
        parts = []
        if n_lo:
            lg = lax.dot_general(ql, kv_parts[0], (((1,), (1,)), ((), ())), preferred_element_type=F32)
            parts.append(lg + bias_ref[:, :n_lo] + srow_ref[h][:, :n_lo])
        lg = lax.dot_general(ql, kv_parts[-1], (((1,), (1,)), ((), ())), preferred_element_type=F32)
        parts.append(lg + bias_ref[:, n_lo:] + diag_ref[h])
        ol = _softmax_pv(parts, kv_parts)
        acc = acc + jnp.dot(ol.astype(BF16), wuv_ref[h], preferred_element_type=F32)
    o_ref[...] = acc.astype(o_ref.dtype)


def _dsa_tile(cq3, ckv3, kidx3, widx3, qt, tq, n_sel, wuq, wuk, wiq, wuv):
    b, s, _ = cq3.shape
    sk = (qt + 1) * tq
    slopes = _alibi_slopes_l2(H_A)
    srow = (slopes[:, None] * jnp.arange(sk, dtype=F32)[None, :]).reshape(H_A, 1, sk)
    diag = _alibi_diag(H_A, tq) + srow[:, :, sk - tq:]
    consts = (wuq, wuk, wiq, wuv, srow, diag)
    return pl.pallas_call(
        functools.partial(_dsa_kernel, q0=qt * tq, n_sel=n_sel),
        grid=(b,),
        in_specs=[pl.BlockSpec((None, tq, D_CQ), lambda i: (i, qt, 0)),
                  pl.BlockSpec((None, sk, D_C), lambda i: (i, 0, 0)),
                  pl.BlockSpec((None, sk, LANES), lambda i: (i, 0, 0)),
                  pl.BlockSpec((None, tq, LANES), lambda i: (i, qt, 0))]
        + [_full(c, 1) for c in consts],
        out_specs=pl.BlockSpec((None, tq, H_A * DH_A), lambda i: (i, 0, 0)),
        out_shape=jax.ShapeDtypeStruct((b, tq, H_A * DH_A), BF16),
        scratch_shapes=[pltpu.VMEM((tq, sk), F32)],
        compiler_params=_cparams(("parallel",)),
        name=f"dsa_q{qt}",
    )(cq3, ckv3, kidx3, widx3, *consts)


def _gelu(x):
    return 0.5 * x * (1.0 + jnp.tanh(0.7978845608028654 * (x + 0.044715 * (x * x * x))))


def _gmlp_kernel(u_ref, v_ref, vg_ref, vb_ref, ws_ref, bias_ref, o_ref):
    tg = u_ref.shape[0]
    ri = lax.broadcasted_iota(I32, (GMLP_CHUNK, GMLP_CHUNK), 0)
    ci = lax.broadcasted_iota(I32, (GMLP_CHUNK, GMLP_CHUNK), 1)
    tril = ri >= ci
    grp = lax.broadcasted_iota(I32, (GMLP_CHUNK, GMLP_WIDTH), 1) // CG_B
    ws = [jnp.where(tril, ws_ref[g], 0.0).astype(BF16) for g in range(G_B)]
    for c in range(tg // GMLP_CHUNK):
        sl = slice(c * GMLP_CHUNK, (c + 1) * GMLP_CHUNK)
        u = _gelu(u_ref[sl, :])
        v = _gelu(v_ref[sl, :])
        mu = jnp.mean(v, axis=-1, keepdims=True)
        vc = v - mu
        var = jnp.mean(vc * vc, axis=-1, keepdims=True)
        vn = (vc * lax.rsqrt(var + LN_EPS) * vg_ref[...] + vb_ref[...]).astype(BF16)
        mixed = bias_ref[...]
        for g in range(G_B):
            r = jnp.dot(ws[g], vn, preferred_element_type=F32)
            mixed = mixed + jnp.where(grp == g, r, 0.0)
        o_ref[sl, :] = (u * mixed).astype(o_ref.dtype)


def _gmlp(uv3, vg, vb, ws, bias):
    b, s, _ = uv3.shape
    tg = min(GMLP_TG, s)
    consts = (vg, vb, ws, bias)
    return pl.pallas_call(
        _gmlp_kernel,
        grid=(b, s // tg),
        in_specs=[pl.BlockSpec((None, tg, GMLP_WIDTH), lambda i, j: (i, j, 0)),
                  pl.BlockSpec((None, tg, GMLP_WIDTH), lambda i, j: (i, j, 1))]
        + [_full(c, 2) for c in consts],
        out_specs=pl.BlockSpec((None, tg, GMLP_WIDTH), lambda i, j: (i, j, 0)),
        out_shape=jax.ShapeDtypeStruct((b, s, GMLP_WIDTH), BF16),
        compiler_params=_cparams(("parallel", "parallel")),
        name="gmlp",
    )(uv3, uv3, *consts)


def _diff_kernel(q_ref, k_ref, v_ref, kpos_ref, qa_ref, diag_ref, lq1_ref, lk1_ref, lq2_ref, lk2_ref,
                 g_ref, o_ref, *, lam_init):
    tq = q_ref.shape[0]
    sk = k_ref.shape[0]
    n_lo = sk - tq
    lam = (jnp.exp(jnp.sum(lq1_ref[...] * lk1_ref[...], axis=-1, keepdims=True))
           - jnp.exp(jnp.sum(lq2_ref[...] * lk2_ref[...], axis=-1, keepdims=True)) + lam_init)
    q = q_ref[...]
    k = k_ref[...]
    v_parts = ([v_ref[:n_lo, :]] if n_lo else []) + [v_ref[n_lo:, :]]
    q_lane = lax.broadcasted_iota(I32, q.shape, 1)
    k_lane = lax.broadcasted_iota(I32, k.shape, 1)
    outs = []
    for m in range(2):
        q_own = (q_lane < DH_C) if m == 0 else (q_lane >= DH_C)
        k_own = (k_lane < DH_C) if m == 0 else (k_lane >= DH_C)
        qm = jnp.where(q_own, q, jnp.broadcast_to(qa_ref[m], q.shape).astype(BF16))
        km = jnp.where(k_own, k, kpos_ref[m])
        parts = []
        if n_lo:
            parts.append(lax.dot_general(qm, km[:n_lo], (((1,), (1,)), ((), ())),
                                         preferred_element_type=F32))
        lg = lax.dot_general(qm, km[n_lo:], (((1,), (1,)), ((), ())), preferred_element_type=F32)
        parts.append(lg + diag_ref[...])
        outs.append(_softmax_pv(parts, v_parts))
    o = outs[0] - lam * outs[1]
    o = o * lax.rsqrt(jnp.mean(o * o, axis=-1, keepdims=True) + LN_EPS) * g_ref[...]
    o_ref[...] = (o * (1.0 - lam_init)).astype(o_ref.dtype)


def _diff_tile(proj3, qt, tq, lam_init, kpos, qa, diag, lq1, lk1, lq2, lk2, g):
    b, s, _ = proj3.shape
    sk = (qt + 1) * tq
    w = 2 * DH_C
    consts = (lq1, lk1, lq2, lk2, g)
    return pl.pallas_call(
        functools.partial(_diff_kernel, lam_init=lam_init),
        grid=(b, H_C),
        in_specs=[pl.BlockSpec((None, tq, w), lambda i, j: (i, qt, j)),
                  pl.BlockSpec((None, sk, w), lambda i, j: (i, 0, H_C + j)),
                  pl.BlockSpec((None, sk, w), lambda i, j: (i, 0, 2 * H_C + j)),
                  pl.BlockSpec((2, sk, w), lambda i, j: (0, 0, 0)),
                  pl.BlockSpec((None, 2, 1, w), lambda i, j: (j, 0, 0, 0)),
                  pl.BlockSpec((None, tq, tq), lambda i, j: (j, 0, 0))]
        + [_full(c, 2) for c in consts],
        out_specs=pl.BlockSpec((None, tq, w), lambda i, j: (i, 0, j)),
        out_shape=jax.ShapeDtypeStruct((b, tq, H_C * w), BF16),
        compiler_params=_cparams(("parallel", "parallel")),
        name=f"diff_q{qt}",
    )(proj3, proj3, proj3, kpos, qa, diag, *consts)


def _pack_halves(x):
    lo = lax.bitcast_convert_type(x[:, :HALF].astype(BF16).astype(F32), I32)
    hi = lax.bitcast_convert_type(x[:, HALF:].astype(BF16).astype(F32), I32)
    return lax.shift_right_logical(lo, 16) | (hi & HI_MASK)


def _unpack_lo(w):
    return lax.bitcast_convert_type(lax.shift_left(w, 16), F32)


def _unpack_hi(w):
    return lax.bitcast_convert_type(w & HI_MASK, F32)


def _layer_norm_rows(z, g, b):
    mu = jnp.mean(z, axis=-1, keepdims=True)
    zc = z - mu
    var = jnp.mean(zc * zc, axis=-1, keepdims=True)
    return zc * lax.rsqrt(var + LN_EPS) * g + b


def _split2(x):
    hi = x.astype(BF16)
    mid = (x - hi.astype(F32)).astype(BF16)
    return hi, mid


def _outln_kernel(*refs, n_in):
    xs = refs[:n_in]
    ws = refs[n_in:2 * n_in]
    (h_ref, g_ref, b_ref, rw_ref, rb_ref, tri_ref,
     h1_ref, hp_ref, te_ref, tg_ref, rk_ref, cnt_ref) = refs[2 * n_in:]
    acc = None
    for x_ref, w_ref in zip(xs, ws):
        d = jnp.dot(x_ref[...], w_ref[...], preferred_element_type=F32)
        acc = d if acc is None else acc + d
    h1 = _layer_norm_rows(DN_ALPHA * h_ref[...] + acc, g_ref[...], b_ref[...])
    h1_ref[...] = h1
    hp_ref[...] = _pack_halves(h1)

    h_hi, h_mid = _split2(h1)
    dotf = lambda a, b: jnp.dot(a, b, preferred_element_type=F32)
    logits = dotf(h_hi, rw_ref[0]) + dotf(h_mid, rw_ref[0]) + dotf(h_hi, rw_ref[1]) + rb_ref[...]
    lane = lax.broadcasted_iota(I32, logits.shape, 1)
    vals, idxs = [], []
    for _ in range(TOP_K):
        mx = jnp.max(logits, axis=-1, keepdims=True)
        ix = jnp.min(jnp.where(logits == mx, lane, LANES), axis=-1, keepdims=True)
        vals.append(mx)
        idxs.append(ix)
        logits = jnp.where(lane == ix, -jnp.inf, logits)
    es = [jnp.exp(v - vals[0]) for v in vals]
    den = es[0] + es[1] + es[2] + es[3]

    @pl.when(pl.program_id(0) == 0)
    def _():
        cnt_ref[...] = jnp.zeros_like(cnt_ref)

    member = jnp.zeros(lane.shape, F32)
    for k in range(TOP_K):
        member = jnp.where(lane == idxs[k], 1.0, member)
    before = jnp.dot(tri_ref[...], member.astype(BF16), preferred_element_type=F32) + cnt_ref[...]
    te = jnp.zeros(lane.shape, I32)
    tg = jnp.zeros(lane.shape, F32)
    rk = jnp.zeros(lane.shape, F32)
    for k in range(TOP_K):
        rank_k = jnp.sum(jnp.where(lane == idxs[k], before, 0.0), axis=-1, keepdims=True)
        te = jnp.where(lane == k, idxs[k], te)
        tg = jnp.where(lane == k, es[k] / den, tg)
        rk = jnp.where(lane == k, rank_k, rk)
    te_ref[...] = te
    tg_ref[...] = tg
    rk_ref[...] = rk.astype(I32)
    cnt_ref[...] += jnp.sum(member, axis=0, keepdims=True)


def _outln(xs, ws, h2, g, b, rw, rb):
    m = h2.shape[0]
    tm = min(PROJ_TM, m)
    n_in = len(xs)
    row = lambda a: pl.BlockSpec((tm, a.shape[1]), lambda i: (i, 0))
    ri = lax.broadcasted_iota(I32, (tm, tm), 0)
    ci = lax.broadcasted_iota(I32, (tm, tm), 1)
    tri = (ci < ri).astype(BF16)
    rw3 = jnp.stack(_split2(rw))
    return pl.pallas_call(
        functools.partial(_outln_kernel, n_in=n_in),
        grid=(m // tm,),
        in_specs=[row(x) for x in xs] + [_full(w, 1) for w in ws]
        + [row(h2), _full(g, 1), _full(b, 1), _full(rw3, 1), _full(rb, 1), _full(tri, 1)],
        out_specs=[pl.BlockSpec((tm, D_MODEL), lambda i: (i, 0)),
                   pl.BlockSpec((tm, HALF), lambda i: (i, 0)),
                   pl.BlockSpec((tm, LANES), lambda i: (i, 0)),
                   pl.BlockSpec((tm, LANES), lambda i: (i, 0)),
                   pl.BlockSpec((tm, LANES), lambda i: (i, 0)),
                   pl.BlockSpec((1, LANES), lambda i: (0, 0))],
        out_shape=[jax.ShapeDtypeStruct((m, D_MODEL), F32),
                   jax.ShapeDtypeStruct((m, HALF), I32),
                   jax.ShapeDtypeStruct((m, LANES), I32),
                   jax.ShapeDtypeStruct((m, LANES), F32),
                   jax.ShapeDtypeStruct((m, LANES), I32),
                   jax.ShapeDtypeStruct((1, LANES), F32)],
        compiler_params=_cparams(("arbitrary",)),
        name="outln_router",
    )(*xs, *ws, h2, g, b, rw3, rb, tri)


def _sc_mesh():
    return plsc.VectorSubcoreMesh(core_axis_name="core", subcore_axis_name="subcore")


def _pad_windows(idx2):
    return jnp.pad(idx2, ((0, 0), (0, LANES - SC_WIN)))


def _sc_scatter(x, pos):
    t, d = x.shape
    nw = t // SC_WIN
    idx = _pad_windows(pos.reshape(nw, SC_WIN, TOP_K).transpose(0, 2, 1).reshape(nw * TOP_K, SC_WIN))

    @pl.kernel(out_type=jax.ShapeDtypeStruct((t * TOP_K, d), x.dtype), mesh=_sc_mesh())
    def scatter_rows(x_hbm, i_hbm, o_hbm):
        def body(x_vmem, i_vmem):
            pltpu.sync_copy(x_vmem, o_hbm.at[i_vmem.at[0, pl.ds(0, SC_WIN)]])

        pltpu.emit_pipeline(
            body,
            grid=(nw * TOP_K,),
            in_specs=[pl.BlockSpec((SC_WIN, d), lambda i: (i // TOP_K, 0)),
                      pl.BlockSpec((1, LANES), lambda i: (i, 0))],
            out_specs=[],
            core_axis_name=("core", "subcore"),
            dimension_semantics=(pltpu.PARALLEL,),
        )(x_hbm, i_hbm)

    return scatter_rows(x, idx)


def _sc_gather(y, idx):
    n = idx.shape[0]
    d = y.shape[1]
    idx2 = _pad_windows(idx.reshape(n // SC_WIN, SC_WIN))

    @pl.kernel(out_type=jax.ShapeDtypeStruct((n, d), y.dtype), mesh=_sc_mesh())
    def gather_rows(y_hbm, i_hbm, o_hbm):
        def body(i_vmem, o_vmem):
            pltpu.sync_copy(y_hbm.at[i_vmem.at[0, pl.ds(0, SC_WIN)]], o_vmem)

        pltpu.emit_pipeline(
            body,
            grid=(n // SC_WIN,),
            in_specs=[pl.BlockSpec((1, LANES), lambda i: (i, 0))],
            out_specs=[pl.BlockSpec((SC_WIN, d), lambda i: (i, 0))],
            core_axis_name=("core", "subcore"),
            dimension_semantics=(pltpu.PARALLEL,),
        )(i_hbm, o_hbm)

    return gather_rows(y, idx2)


def _ffn_kernel(vb_ref, ve_ref, lo_ref, hi_ref, first_ref, x_ref, wgu_ref, bgu_ref, wdn_ref, bdn_ref,
                y_ref):
    v = pl.program_id(0)
    lo = lo_ref[v]
    hi = hi_ref[v]

    @pl.when(hi > lo)
    def _():
        xw = x_ref[...]
        xl = _unpack_lo(xw).astype(BF16)
        xh = _unpack_hi(xw).astype(BF16)
        acc = jnp.zeros((x_ref.shape[0], D_MODEL), F32)
        for c in range(D_FF // FFN_FC):
            gs = slice(c * FFN_FC, (c + 1) * FFN_FC)
            us = slice(D_FF + c * FFN_FC, D_FF + (c + 1) * FFN_FC)
            gate = (jnp.dot(xl, wgu_ref[:HALF, gs], preferred_element_type=F32)
                    + jnp.dot(xh, wgu_ref[HALF:, gs], preferred_element_type=F32) + bgu_ref[:, gs])
            up = (jnp.dot(xl, wgu_ref[:HALF, us], preferred_element_type=F32)
                  + jnp.dot(xh, wgu_ref[HALF:, us], preferred_element_type=F32) + bgu_ref[:, us])
            gate = jnp.minimum(gate, SWIGLU_LIMIT)
            up = jnp.clip(up, -SWIGLU_LIMIT, SWIGLU_LIMIT)
            act = (up + 1.0) * (gate * jax.nn.sigmoid(gate * SWIGLU_ALPHA))
            acc = acc + jnp.dot(act.astype(BF16), wdn_ref[gs, :], preferred_element_type=F32)
        new = _pack_halves(acc + bdn_ref[...])
        row = lax.broadcasted_iota(I32, new.shape, 0)
        keep = jnp.where(first_ref[v] == 1, jnp.zeros_like(new), y_ref[...])
        y_ref[...] = jnp.where(row >= lo, jnp.where(row < hi, new, keep), keep)


def _ffn(visits, xs, wgu, bgu, wdn, bdn):
    n_rows, hw = xs.shape
    n_vis = visits[0].shape[0]
    blk = lambda v, vb, ve, lo, hi, fi: (vb[v], 0)
    exp = lambda v, vb, ve, lo, hi, fi: (ve[v], 0, 0)
    grid_spec = pltpu.PrefetchScalarGridSpec(
        num_scalar_prefetch=5,
        grid=(n_vis,),
        in_specs=[pl.BlockSpec((FFN_TM, hw), blk),
                  pl.BlockSpec((None, D_MODEL, 2 * D_FF), exp),
                  pl.BlockSpec((None, 1, 2 * D_FF), exp),
                  pl.BlockSpec((None, D_FF, D_MODEL), exp),
                  pl.BlockSpec((None, 1, D_MODEL), exp)],
        out_specs=pl.BlockSpec((FFN_TM, hw), blk),
    )
    return pl.pallas_call(
        _ffn_kernel,
        grid_spec=grid_spec,
        out_shape=jax.ShapeDtypeStruct((n_rows, hw), I32),
        compiler_params=_cparams(("arbitrary",)),
        name="moe_ffn",
    )(*visits, xs, wgu, bgu, wdn, bdn)


def _combine_kernel(yg_ref, h_ref, gate_ref, g_ref, b_ref, o_ref):
    gate = gate_ref[...]
    h = h_ref[...]
    zl = DN_ALPHA * h[:, :HALF]
    zh = DN_ALPHA * h[:, HALF:]
    for k in range(TOP_K):
        w = yg_ref[k]
        gk = gate[:, k:k + 1]
        zl = zl + gk * _unpack_lo(w)
        zh = zh + gk * _unpack_hi(w)
    mu = (jnp.sum(zl, axis=-1, keepdims=True) + jnp.sum(zh, axis=-1, keepdims=True)) * (1.0 / D_MODEL)
    zl = zl - mu
    zh = zh - mu
    var = (jnp.sum(zl * zl, axis=-1, keepdims=True)
           + jnp.sum(zh * zh, axis=-1, keepdims=True)) * (1.0 / D_MODEL)
    r = lax.rsqrt(var + LN_EPS)
    o_ref[:, :HALF] = zl * r * g_ref[:, :HALF] + b_ref[:, :HALF]
    o_ref[:, HALF:] = zh * r * g_ref[:, HALF:] + b_ref[:, HALF:]


def _combine(yg, h1, gates, g, b):
    m, d = h1.shape
    tm = min(COMB_TM, m)
    return pl.pallas_call(
        _combine_kernel,
        grid=(m // tm,),
        in_specs=[pl.BlockSpec((TOP_K, tm, HALF), lambda i: (0, i, 0)),
                  pl.BlockSpec((tm, d), lambda i: (i, 0)),
                  pl.BlockSpec((tm, LANES), lambda i: (i, 0)),
                  _full(g, 1), _full(b, 1)],
        out_specs=pl.BlockSpec((tm, d), lambda i: (i, 0)),
        out_shape=jax.ShapeDtypeStruct((m, d), F32),
        compiler_params=_cparams(("parallel",)),
        name="moe_combine",
    )(yg, h1, gates, g, b)


def _routing_tables(top_e, rank, counts):
    n_tok = top_e.shape[0]
    n_blocks = n_tok * TOP_K // FFN_TM
    n_vis = n_blocks + N_EXPERTS - 1
    end = jnp.cumsum(counts)
    start = end - counts
    onehot = top_e[:, :, None] == jnp.arange(N_EXPERTS, dtype=I32)[None, None, :]
    pos = (rank + jnp.sum(jnp.where(onehot, start[None, None, :], 0), axis=-1)).astype(I32)

    first = start // FFN_TM
    last = jnp.maximum(end - 1, 0) // FFN_TM
    nvis = jnp.where(counts > 0, last - first + 1, 0)
    vend = jnp.cumsum(nvis)
    vstart = vend - nvis
    total = vend[-1]
    v = jnp.arange(n_vis, dtype=I32)
    valid = v < total
    ev = jnp.minimum(jnp.searchsorted(vend, v, side="right"), N_EXPERTS - 1).astype(I32)
    ev = jnp.where(valid, ev, ev[jnp.maximum(total - 1, 0)])
    bv = jnp.where(valid, first[ev] + v - vstart[ev], n_blocks - 1).astype(I32)
    lo = jnp.where(valid, jnp.maximum(start[ev], bv * FFN_TM) - bv * FFN_TM, 0).astype(I32)
    hi = jnp.where(valid, jnp.minimum(end[ev], (bv + 1) * FFN_TM) - bv * FFN_TM, 0).astype(I32)
    prev_b = jnp.concatenate([jnp.full((1,), -1, I32), bv[:-1]])
    fi = jnp.where(valid & (bv != prev_b), 1, 0).astype(I32)
    return pos, (bv, ev, lo, hi, fi)


def _moe(routed, wgu, bgu, wdn, bdn, g, b):
    h1, h1p, te, gates, rk, cnt = routed
    n_tok = h1.shape[0]
    pos, visits = _routing_tables(te[:, :TOP_K], rk[:, :TOP_K], cnt[0, :N_EXPERTS].astype(I32))
    xs = _sc_scatter(h1p, pos)
    y = _ffn(visits, xs, wgu, bgu, wdn, bdn)
    yg = _sc_gather(y, jnp.transpose(pos).reshape(-1)).reshape(TOP_K, n_tok, HALF)
    return _combine(yg, h1, gates, g, b)


def _row(a):
    return a.reshape(1, -1).astype(F32)


def _even_layer(h2, bsz, seq, w_in, q_norm, kv_norm, w_uq, w_uk, w_uv, w_iq, idx_g, idx_b, v_g, v_b,
                w_s, b_s, w_out, ln_g, ln_b, rw, rb):
    o4 = D_CQ + D_C + D_IDX + H_IDX
    w_in_p = jnp.concatenate(
        [w_in[:, :o4], jnp.zeros((D_MODEL, E_IN_PAD - E_IN_EVEN), F32), w_in[:, o4:]], axis=1)
    pad_idx = lambda a: jnp.pad(_row(a), ((0, 0), (0, LANES - D_IDX)))
    cq, ckv, kidx, widx, uv = _proj_even(h2, w_in_p.astype(BF16), _row(q_norm), _row(kv_norm),
                                         pad_idx(idx_g), pad_idx(idx_b))
    as3 = lambda a: a.reshape(bsz, seq, a.shape[-1])

    n_sel = min(TOPK_MAX, seq // 4)
    tq = min(DSA_TQ, seq)
    wuq = jnp.transpose(w_uq, (1, 0, 2)).astype(BF16)
    wiq = jnp.pad(jnp.transpose(w_iq, (1, 0, 2)),
                  ((0, 0), (0, 0), (0, LANES - D_IDX))).astype(BF16)
    wuk = w_uk.astype(BF16)
    eye = jnp.eye(H_A, dtype=F32)
    wuv = (w_uv[:, :, None, :] * eye[:, None, :, None]).reshape(H_A, D_C, H_A * DH_A).astype(BF16)
    o_a = jnp.concatenate(
        [_dsa_tile(as3(cq), as3(ckv), as3(kidx), as3(widx), qt, tq, n_sel, wuq, wuk, wiq, wuv)
         for qt in range(seq // tq)], axis=1)

    gbias = jnp.repeat(jnp.transpose(b_s), CG_B, axis=1)
    o_b = _gmlp(as3(uv), _row(v_g), _row(v_b), w_s, gbias)

    n_a = H_A * DH_A
    w_out_b = w_out.astype(BF16)
    return _outln([o_a.reshape(bsz * seq, n_a), o_b.reshape(bsz * seq, GMLP_WIDTH)],
                  [w_out_b[:n_a], w_out_b[n_a:]], h2, _row(ln_g), _row(ln_b), rw, rb)


def _odd_layer(h2, bsz, seq, layer, w_in, lq1, lk1, lq2, lk2, subln_g, w_out, ln_g, ln_b, rw, rb):
    lam_init = 0.8 - 0.6 * math.exp(-0.3 * layer)
    proj3 = _proj_odd(h2, w_in.astype(BF16)).reshape(bsz, seq, 3 * QK_W_C)
    tq = min(DIFF_TQ, seq)
    kpos = jnp.stack([_alibi_key_cols(seq, DH_C), _alibi_key_cols(seq, 0)])
    qa0 = _alibi_query_cols(H_C)
    qa = jnp.stack([jnp.roll(qa0, DH_C, axis=-1), qa0], axis=1)
    diag = _alibi_diag(H_C, tq)
    o = jnp.concatenate(
        [_diff_tile(proj3, qt, tq, lam_init, kpos, qa, diag, _row(lq1), _row(lk1), _row(lq2), _row(lk2),
                    _row(subln_g)) for qt in range(seq // tq)], axis=1)
    return _outln([o.reshape(bsz * seq, QK_W_C)], [w_out.astype(BF16)], h2, _row(ln_g), _row(ln_b),
                  rw, rb)


def kernel(x, ev_w_in, ev_q_norm, ev_kv_norm, ev_w_uq, ev_w_uk, ev_w_uv, ev_w_iq, ev_idx_k_g, ev_idx_k_b, ev_v_norm_g, ev_v_norm_b, ev_w_s, ev_b_s, ev_w_out, od_w_in, od_lambda_q1, od_lambda_k1, od_lambda_q2, od_lambda_k2, od_subln_g, od_w_out, ln1_g, ln1_b, ln2_g, ln2_b, router_w, router_b, exp_w_gu, exp_b_gu, exp_w_dn, exp_b_dn):
    bsz, seq, d = x.shape
    h2 = x.reshape(bsz * seq, d)
    for l in range(DEPTH):
        j = l // 2
        rw = jnp.pad(router_w[l], ((0, 0), (0, LANES - N_EXPERTS)))
        rb = jnp.pad(router_b[l], (0, LANES - N_EXPERTS), constant_values=-jnp.inf).reshape(1, LANES)
        if l % 2 == 0:
            routed = _even_layer(h2, bsz, seq, ev_w_in[j], ev_q_norm[j], ev_kv_norm[j],
                                 ev_w_uq[j], ev_w_uk[j], ev_w_uv[j], ev_w_iq[j], ev_idx_k_g[j],
                                 ev_idx_k_b[j], ev_v_norm_g[j], ev_v_norm_b[j], ev_w_s[j],
                                 ev_b_s[j], ev_w_out[j], ln1_g[l], ln1_b[l], rw, rb)
        else:
            routed = _odd_layer(h2, bsz, seq, l, od_w_in[j], od_lambda_q1[j], od_lambda_k1[j],
                                od_lambda_q2[j], od_lambda_k2[j], od_subln_g[j], od_w_out[j],
                                ln1_g[l], ln1_b[l], rw, rb)
        h2 = _moe(routed,
                  exp_w_gu[l].astype(BF16), exp_b_gu[l][:, None, :],
                  exp_w_dn[l].astype(BF16), exp_b_dn[l][:, None, :],
                  _row(ln2_g[l]), _row(ln2_b[l]))
    return h2.reshape(bsz, seq, d)
```

```python
import functools
import math

import jax
import jax.numpy as jnp
from jax import lax
from jax.experimental import pallas as pl
from jax.experimental.pallas import tpu as pltpu
from jax.experimental.pallas import tpu_sc as plsc

F32 = jnp.float32
BF16 = jnp.bfloat16
I32 = jnp.int32

D_MODEL = 1024
DEPTH = 4
CHUNK = 64
CHUNK_SHIFT = 6
H_A = 8
DH_A = 64
D_CQ = 256
D_C = 128
H_IDX = 4
D_IDX = 64
TOPK_MAX = 256
GMLP_CHUNK = 128
G_B = 8
GMLP_WIDTH = 512
CG_B = GMLP_WIDTH // G_B
H_C = 8
DH_C = 64
QK_W_C = H_C * 2 * DH_C
N_EXPERTS = 32
TOP_K = 4
D_FF = 1024
SWIGLU_ALPHA = 1.702
SWIGLU_LIMIT = 7.0
DN_ALPHA = (2 * DEPTH) ** 0.25
E_IN_EVEN = D_CQ + D_C + D_IDX + H_IDX + 2 * GMLP_WIDTH
LN_EPS = 1e-5
NEG = -1e30
LOG2E = 1.4426950408889634

LANES = 128
E_IN_PAD = 1536
INT_MIN = -(2 ** 31)
HALF = D_MODEL // 2
HI_MASK = -65536
SC_WIN = 64
POS_RADIX = 256
N_POS_COLS = 6

PROJ_TM = 512
DSA_TQ = 256
DIFF_TQ = 512
GMLP_TG = 512
FFN_TM = 512
FFN_FC = 512
COMB_TM = 512
VMEM_LIMIT = 56 * 1024 * 1024


def _cparams(sem):
    return pltpu.CompilerParams(dimension_semantics=sem, vmem_limit_bytes=VMEM_LIMIT)


def _full(a, n_grid):
    zeros = (0,) * a.ndim
    return pl.BlockSpec(a.shape, lambda *_: zeros)


def _alibi_slopes_l2(n):
    return jnp.exp2(-8.0 * jnp.arange(1, n + 1, dtype=F32) / n) * LOG2E


def _alibi_query_cols(n):
    s = _alibi_slopes_l2(n)
    hi = s.astype(BF16).astype(F32)
    mid = (s - hi).astype(BF16).astype(F32)
    lo = (s - hi - mid).astype(BF16).astype(F32)
    cols = jnp.stack([hi * POS_RADIX, hi, mid * POS_RADIX, mid, lo * POS_RADIX, lo], axis=-1)
    return jnp.pad(cols, ((0, 0), (0, LANES - N_POS_COLS))).reshape(n, 1, LANES)


def _alibi_key_cols(seq, lane0):
    s = jnp.arange(seq, dtype=I32)
    a = (s // POS_RADIX).astype(F32)
    b = (s % POS_RADIX).astype(F32)
    cols = jnp.stack([a, b, a, b, a, b], axis=-1)
    return jnp.pad(cols, ((0, 0), (lane0, LANES - N_POS_COLS - lane0))).astype(BF16)


def _alibi_diag(n, tq):
    t = jnp.arange(tq, dtype=I32)[:, None]
    s = jnp.arange(tq, dtype=I32)[None, :]
    ok = (s >> CHUNK_SHIFT) <= (t >> CHUNK_SHIFT)
    ahead = jnp.maximum(s - t, 0).astype(F32)
    corr = -2.0 * _alibi_slopes_l2(n)[:, None, None] * ahead[None]
    return jnp.where(ok[None], corr, NEG)


def _proj_even_kernel(x_ref, w_ref, qg_ref, kvg_ref, ig_ref, ib_ref,
                      cq_ref, ckv_ref, kidx_ref, widx_ref, uv_ref):
    acc = jnp.dot(x_ref[...].astype(BF16), w_ref[...], preferred_element_type=F32)
    cq = acc[:, :D_CQ]
    cq_ref[...] = (cq * lax.rsqrt(jnp.mean(cq * cq, axis=-1, keepdims=True) + LN_EPS)
                   * qg_ref[...]).astype(cq_ref.dtype)
    kv = acc[:, D_CQ:D_CQ + D_C]
    ckv_ref[...] = (kv * lax.rsqrt(jnp.mean(kv * kv, axis=-1, keepdims=True) + LN_EPS)
                    * kvg_ref[...]).astype(ckv_ref.dtype)
    blk = acc[:, D_CQ + D_C:D_CQ + D_C + LANES]
    lane = lax.broadcasted_iota(I32, blk.shape, 1)
    is_k = lane < D_IDX
    mu = jnp.sum(jnp.where(is_k, blk, 0.0), axis=-1, keepdims=True) * (1.0 / D_IDX)
    kc = jnp.where(is_k, blk - mu, 0.0)
    var = jnp.sum(kc * kc, axis=-1, keepdims=True) * (1.0 / D_IDX)
    kidx_ref[...] = jnp.where(is_k, kc * lax.rsqrt(var + LN_EPS) * ig_ref[...] + ib_ref[...],
                              0.0).astype(kidx_ref.dtype)
    widx_ref[...] = jnp.where(is_k, 0.0, blk * (H_IDX ** -0.5))
    uv_ref[...] = acc[:, E_IN_PAD - 2 * GMLP_WIDTH:]


def _proj_even(x2, w, qg, kvg, ig, ib):
    m, k = x2.shape
    tm = min(PROJ_TM, m)
    row = lambda n: pl.BlockSpec((tm, n), lambda i: (i, 0))
    consts = (qg, kvg, ig, ib)
    return pl.pallas_call(
        _proj_even_kernel,
        grid=(m // tm,),
        in_specs=[row(k), _full(w, 1)] + [_full(c, 1) for c in consts],
        out_specs=[row(D_CQ), row(D_C), row(LANES), row(LANES), row(2 * GMLP_WIDTH)],
        out_shape=[jax.ShapeDtypeStruct((m, D_CQ), BF16),
                   jax.ShapeDtypeStruct((m, D_C), BF16),
                   jax.ShapeDtypeStruct((m, LANES), BF16),
                   jax.ShapeDtypeStruct((m, LANES), F32),
                   jax.ShapeDtypeStruct((m, 2 * GMLP_WIDTH), F32)],
        compiler_params=_cparams(("parallel",)),
        name="proj_even",
    )(x2, w, *consts)


def _proj_odd_kernel(x_ref, w_ref, o_ref):
    acc = jnp.dot(x_ref[...].astype(BF16), w_ref[...], preferred_element_type=F32)
    o_ref[:, :QK_W_C] = (acc[:, :QK_W_C] * (DH_C ** -0.5 * LOG2E)).astype(o_ref.dtype)
    o_ref[:, QK_W_C:] = acc[:, QK_W_C:].astype(o_ref.dtype)


def _proj_odd(x2, w):
    m, k = x2.shape
    n = w.shape[1]
    tm = min(PROJ_TM, m)
    return pl.pallas_call(
        _proj_odd_kernel,
        grid=(m // tm,),
        in_specs=[pl.BlockSpec((tm, k), lambda i: (i, 0)), _full(w, 1)],
        out_specs=pl.BlockSpec((tm, n), lambda i: (i, 0)),
        out_shape=jax.ShapeDtypeStruct((m, n), BF16),
        compiler_params=_cparams(("parallel",)),
        name="proj_odd",
    )(x2, w)


def _softmax_pv(parts, kv_parts):
    m = None
    for lg in parts:
        mx = jnp.max(lg, axis=-1, keepdims=True)
        m = mx if m is None else jnp.maximum(m, mx)
    l = None
    o = None
    for lg, kv in zip(parts, kv_parts):
        p = jnp.exp2(lg - m)
        s = jnp.sum(p, axis=-1, keepdims=True)
        d = jnp.dot(p.astype(BF16), kv, preferred_element_type=F32)
        l = s if l is None else l + s
        o = d if o is None else o + d
    return o * (1.0 / l)


def _dsa_kernel(cq_ref, ckv_ref, kidx_ref, widx_ref, wuq_ref, wuk_ref, wiq_ref, wuv_ref, srow_ref,
                diag_ref, o_ref, bias_ref, *, q0, n_sel):
    tq = cq_ref.shape[0]
    sk = ckv_ref.shape[0]
    cqb = cq_ref[...]
    kidx = kidx_ref[...]
    widx = widx_ref[...]

    score = None
    for h in range(H_IDX):
        qi = jnp.dot(cqb, wiq_ref[h], preferred_element_type=F32).astype(BF16)
        r = lax.dot_general(qi, kidx, (((1,), (1,)), ((), ())), preferred_element_type=F32)
        term = jnp.maximum(r * (D_IDX ** -0.5), 0.0) * widx[:, D_IDX + h:D_IDX + h + 1]
        score = term if score is None else score + term
    t_pos = q0 + lax.broadcasted_iota(I32, (tq, sk), 0)
    s_pos = lax.broadcasted_iota(I32, (tq, sk), 1)
    allowed = (s_pos >> CHUNK_SHIFT) <= (t_pos >> CHUNK_SHIFT)
    score = jnp.where(allowed, score, NEG)

    bits = lax.bitcast_convert_type(score, I32)
    key = jnp.where(bits < 0, bits ^ 0x7FFFFFFF, bits)
    key = jnp.where(key == -1, 0, key)

    def count(mask):
        return jnp.sum(jnp.where(mask, 1.0, 0.0), axis=-1, keepdims=True)

    nsel = float(n_sel)
    cur = jnp.where(count(key >= 0) >= nsel, 0, INT_MIN).astype(I32)

    def search(i, cur):
        cand = cur | lax.shift_left(jnp.int32(1), 30 - i)
        return jnp.where(count(key >= cand) >= nsel, cand, cur)

    cur = lax.fori_loop(0, 31, search, cur)

    need = nsel - count(key > cur)
    ri = lax.broadcasted_iota(I32, (LANES, LANES), 0)
    ci = lax.broadcasted_iota(I32, (LANES, LANES), 1)
    tri = jnp.where(ri < ci, 1.0, 0.0).astype(BF16)
    ones_b = jnp.ones((LANES, LANES), BF16)
    off = jnp.zeros((tq, 1), F32)
    for c in range(sk // LANES):
        sl = slice(c * LANES, (c + 1) * LANES)
        keyc = key[:, sl]
        eqc = jnp.where(keyc == cur, 1.0, 0.0)
        eqb = eqc.astype(BF16)
        rank = jnp.dot(eqb, tri, preferred_element_type=F32) + off
        sel = jnp.where(keyc > cur, 1.0, jnp.where(rank < need, eqc, 0.0))
        bias_ref[:, sl] = jnp.where(sel > 0.5, 0.0, NEG)
        off = off + jnp.dot(eqb, ones_b, preferred_element_type=F32)[:, :1]

    n_lo = sk - tq
    kv_parts = ([ckv_ref[:n_lo, :]] if n_lo else []) + [ckv_ref[n_lo:, :]]
    acc = jnp.zeros((tq, H_A * DH_A), F32)
    for h in range(H_A):
        qh = jnp.dot(cqb, wuq_ref[h], preferred_element_type=F32).astype(BF16)
        ql = (jnp.dot(qh, wuk_ref[h], preferred_element_type=F32)
              * (DH_A ** -0.5 * LOG2E)).astype(BF16)
        parts = []
        if n_lo:
            lg = lax.dot_general(ql, kv_parts[0], (((1,), (1,)), ((), ())), preferred_element_type=F32)
            parts.append(lg + bias_ref[:, :n_lo] + srow_ref[h][:, :n_lo])
        lg = lax.dot_general(ql, kv_parts[-1], (((1,), (1,)), ((), ())), preferred_element_type=F32)
        parts.append(lg + bias_ref[:, n_lo:] + diag_ref[h])
        ol = _softmax_pv(parts, kv_parts)
        acc = acc + jnp.dot(ol.astype(BF16), wuv_ref[h], preferred_element_type=F32)
    o_ref[...] = acc.astype(o_ref.dtype)


def _dsa_tile(cq3, ckv3, kidx3, widx3, qt, tq, n_sel, wuq, wuk, wiq, wuv):
    b, s, _ = cq3.shape
    sk = (qt + 1) * tq
    slopes = _alibi_slopes_l2(H_A)
    srow = (slopes[:, None] * jnp.arange(sk, dtype=F32)[None, :]).reshape(H_A, 1, sk)
    diag = _alibi_diag(H_A, tq) + srow[:, :, sk - tq:]
    consts = (wuq, wuk, wiq, wuv, srow, diag)
    return pl.pallas_call(
        functools.partial(_dsa_kernel, q0=qt * tq, n_sel=n_sel),
        grid=(b,),
        in_specs=[pl.BlockSpec((None, tq, D_CQ), lambda i: (i, qt, 0)),
                  pl.BlockSpec((None, sk, D_C), lambda i: (i, 0, 0)),
                  pl.BlockSpec((None, sk, LANES), lambda i: (i, 0, 0)),
                  pl.BlockSpec((None, tq, LANES), lambda i: (i, qt, 0))]
        + [_full(c, 1) for c in consts],
        out_specs=pl.BlockSpec((None, tq, H_A * DH_A), lambda i: (i, 0, 0)),
        out_shape=jax.ShapeDtypeStruct((b, tq, H_A * DH_A), BF16),
        scratch_shapes=[pltpu.VMEM((tq, sk), F32)],
        compiler_params=_cparams(("parallel",)),
        name=f"dsa_q{qt}",
    )(cq3, ckv3, kidx3, widx3, *consts)


def _gelu(x):
    return 0.5 * x * (1.0 + jnp.tanh(0.7978845608028654 * (x + 0.044715 * (x * x * x))))


def _gmlp_kernel(u_ref, v_ref, vg_ref, vb_ref, ws_ref, bias_ref, o_ref):
    tg = u_ref.shape[0]
    ri = lax.broadcasted_iota(I32, (GMLP_CHUNK, GMLP_CHUNK), 0)
    ci = lax.broadcasted_iota(I32, (GMLP_CHUNK, GMLP_CHUNK), 1)
    tril = ri >= ci
    grp = lax.broadcasted_iota(I32, (GMLP_CHUNK, GMLP_WIDTH), 1) // CG_B
    ws = [jnp.where(tril, ws_ref[g], 0.0).astype(BF16) for g in range(G_B)]
    for c in range(tg // GMLP_CHUNK):
        sl = slice(c * GMLP_CHUNK, (c + 1) * GMLP_CHUNK)
        u = _gelu(u_ref[sl, :])
        v = _gelu(v_ref[sl, :])
        mu = jnp.mean(v, axis=-1, keepdims=True)
        vc = v - mu
        var = jnp.mean(vc * vc, axis=-1, keepdims=True)
        vn = (vc * lax.rsqrt(var + LN_EPS) * vg_ref[...] + vb_ref[...]).astype(BF16)
        mixed = bias_ref[...]
        for g in range(G_B):
            r = jnp.dot(ws[g], vn, preferred_element_type=F32)
            mixed = mixed + jnp.where(grp == g, r, 0.0)
        o_ref[sl, :] = (u * mixed).astype(o_ref.dtype)


def _gmlp(uv3, vg, vb, ws, bias):
    b, s, _ = uv3.shape
    tg = min(GMLP_TG, s)
    consts = (vg, vb, ws, bias)
    return pl.pallas_call(
        _gmlp_kernel,
        grid=(b, s // tg),
        in_specs=[pl.BlockSpec((None, tg, GMLP_WIDTH), lambda i, j: (i, j, 0)),
                  pl.BlockSpec((None, tg, GMLP_WIDTH), lambda i, j: (i, j, 1))]
        + [_full(c, 2) for c in consts],
        out_specs=pl.BlockSpec((None, tg, GMLP_WIDTH), lambda i, j: (i, j, 0)),
        out_shape=jax.ShapeDtypeStruct((b, s, GMLP_WIDTH), BF16),
        compiler_params=_cparams(("parallel", "parallel")),
        name="gmlp",
    )(uv3, uv3, *consts)


def _diff_kernel(q_ref, k_ref, v_ref, kpos_ref, qa_ref, diag_ref, lq1_ref, lk1_ref, lq2_ref, lk2_ref,
                 g_ref, o_ref, *, lam_init):
    tq = q_ref.shape[0]
    sk = k_ref.shape[0]
    n_lo = sk - tq
    lam = (jnp.exp(jnp.sum(lq1_ref[...] * lk1_ref[...], axis=-1, keepdims=True))
           - jnp.exp(jnp.sum(lq2_ref[...] * lk2_ref[...], axis=-1, keepdims=True)) + lam_init)
    q = q_ref[...]
    k = k_ref[...]
    v_parts = ([v_ref[:n_lo, :]] if n_lo else []) + [v_ref[n_lo:, :]]
    q_lane = lax.broadcasted_iota(I32, q.shape, 1)
    k_lane = lax.broadcasted_iota(I32, k.shape, 1)
    outs = []
    for m in range(2):
        q_own = (q_lane < DH_C) if m == 0 else (q_lane >= DH_C)
        k_own = (k_lane < DH_C) if m == 0 else (k_lane >= DH_C)
        qm = jnp.where(q_own, q, jnp.broadcast_to(qa_ref[m], q.shape).astype(BF16))
        km = jnp.where(k_own, k, kpos_ref[m])
        parts = []
        if n_lo:
            parts.append(lax.dot_general(qm, km[:n_lo], (((1,), (1,)), ((), ())),
                                         preferred_element_type=F32))
        lg = lax.dot_general(qm, km[n_lo:], (((1,), (1,)), ((), ())), preferred_element_type=F32)
        parts.append(lg + diag_ref[...])
        outs.append(_softmax_pv(parts, v_parts))
    o = outs[0] - lam * outs[1]
    o = o * lax.rsqrt(jnp.mean(o * o, axis=-1, keepdims=True) + LN_EPS) * g_ref[...]
    o_ref[...] = (o * (1.0 - lam_init)).astype(o_ref.dtype)


def _diff_tile(proj3, qt, tq, lam_init, kpos, qa, diag, lq1, lk1, lq2, lk2, g):
    b, s, _ = proj3.shape
    sk = (qt + 1) * tq
    w = 2 * DH_C
    consts = (lq1, lk1, lq2, lk2, g)
    return pl.pallas_call(
        functools.partial(_diff_kernel, lam_init=lam_init),
        grid=(b, H_C),
        in_specs=[pl.BlockSpec((None, tq, w), lambda i, j: (i, qt, j)),
                  pl.BlockSpec((None, sk, w), lambda i, j: (i, 0, H_C + j)),
                  pl.BlockSpec((None, sk, w), lambda i, j: (i, 0, 2 * H_C + j)),
                  pl.BlockSpec((2, sk, w), lambda i, j: (0, 0, 0)),
                  pl.BlockSpec((None, 2, 1, w), lambda i, j: (j, 0, 0, 0)),
                  pl.BlockSpec((None, tq, tq), lambda i, j: (j, 0, 0))]
        + [_full(c, 2) for c in consts],
        out_specs=pl.BlockSpec((None, tq, w), lambda i, j: (i, 0, j)),
        out_shape=jax.ShapeDtypeStruct((b, tq, H_C * w), BF16),
        compiler_params=_cparams(("parallel", "parallel")),
        name=f"diff_q{qt}",
    )(proj3, proj3, proj3, kpos, qa, diag, *consts)


def _pack_halves(x):
    lo = lax.bitcast_convert_type(x[:, :HALF].astype(BF16).astype(F32), I32)
    hi = lax.bitcast_convert_type(x[:, HALF:].astype(BF16).astype(F32), I32)
    return lax.shift_right_logical(lo, 16) | (hi & HI_MASK)


def _unpack_lo(w):
    return lax.bitcast_convert_type(lax.shift_left(w, 16), F32)


def _unpack_hi(w):
    return lax.bitcast_convert_type(w & HI_MASK, F32)


def _layer_norm_rows(z, g, b):
    mu = jnp.mean(z, axis=-1, keepdims=True)
    zc = z - mu
    var = jnp.mean(zc * zc, axis=-1, keepdims=True)
    return zc * lax.rsqrt(var + LN_EPS) * g + b


def _split2(x):
    hi = x.astype(BF16)
    mid = (x - hi.astype(F32)).astype(BF16)
    return hi, mid


def _outln_kernel(*refs, n_in):
    xs = refs[:n_in]
    ws = refs[n_in:2 * n_in]
    (h_ref, g_ref, b_ref, rw_ref, rb_ref, tri_ref,
     h1_ref, hp_ref, te_ref, tg_ref, rk_ref, cnt_ref) = refs[2 * n_in:]
    acc = None
    for x_ref, w_ref in zip(xs, ws):
        d = jnp.dot(x_ref[...], w_ref[...], preferred_element_type=F32)
        acc = d if acc is None else acc + d
    h1 = _layer_norm_rows(DN_ALPHA * h_ref[...] + acc, g_ref[...], b_ref[...])
    h1_ref[...] = h1
    hp_ref[...] = _pack_halves(h1)

    h_hi, h_mid = _split2(h1)
    dotf = lambda a, b: jnp.dot(a, b, preferred_element_type=F32)
    logits = dotf(h_hi, rw_ref[0]) + dotf(h_mid, rw_ref[0]) + dotf(h_hi, rw_ref[1]) + rb_ref[...]
    lane = lax.broadcasted_iota(I32, logits.shape, 1)
    vals, idxs = [], []
    for _ in range(TOP_K):
        mx = jnp.max(logits, axis=-1, keepdims=True)
        ix = jnp.min(jnp.where(logits == mx, lane, LANES), axis=-1, keepdims=True)
        vals.append(mx)
        idxs.append(ix)
        logits = jnp.where(lane == ix, -jnp.inf, logits)
    es = [jnp.exp(v - vals[0]) for v in vals]
    den = es[0] + es[1] + es[2] + es[3]

    @pl.when(pl.program_id(0) == 0)
    def _():
        cnt_ref[...] = jnp.zeros_like(cnt_ref)

    member = jnp.zeros(lane.shape, F32)
    for k in range(TOP_K):
        member = jnp.where(lane == idxs[k], 1.0, member)
    before = jnp.dot(tri_ref[...], member.astype(BF16), preferred_element_type=F32) + cnt_ref[...]
    te = jnp.zeros(lane.shape, I32)
    tg = jnp.zeros(lane.shape, F32)
    rk = jnp.zeros(lane.shape, F32)
    for k in range(TOP_K):
        rank_k = jnp.sum(jnp.where(lane == idxs[k], before, 0.0), axis=-1, keepdims=True)
        te = jnp.where(lane == k, idxs[k], te)
        tg = jnp.where(lane == k, es[k] / den, tg)
        rk = jnp.where(lane == k, rank_k, rk)
    te_ref[...] = te
    tg_ref[...] = tg
    rk_ref[...] = rk.astype(I32)
    cnt_ref[...] += jnp.sum(member, axis=0, keepdims=True)


def _outln(xs, ws, h2, g, b, rw, rb):
    m = h2.shape[0]
    tm = min(PROJ_TM, m)
    n_in = len(xs)
    row = lambda a: pl.BlockSpec((tm, a.shape[1]), lambda i: (i, 0))
    ri = lax.broadcasted_iota(I32, (tm, tm), 0)
    ci = lax.broadcasted_iota(I32, (tm, tm), 1)
    tri = (ci < ri).astype(BF16)
    rw3 = jnp.stack(_split2(rw))
    return pl.pallas_call(
        functools.partial(_outln_kernel, n_in=n_in),
        grid=(m // tm,),
        in_specs=[row(x) for x in xs] + [_full(w, 1) for w in ws]
        + [row(h2), _full(g, 1), _full(b, 1), _full(rw3, 1), _full(rb, 1), _full(tri, 1)],
        out_specs=[pl.BlockSpec((tm, D_MODEL), lambda i: (i, 0)),
                   pl.BlockSpec((tm, HALF), lambda i: (i, 0)),
                   pl.BlockSpec((tm, LANES), lambda i: (i, 0)),
                   pl.BlockSpec((tm, LANES), lambda i: (i, 0)),
                   pl.BlockSpec((tm, LANES), lambda i: (i, 0)),
                   pl.BlockSpec((1, LANES), lambda i: (0, 0))],
        out_shape=[jax.ShapeDtypeStruct((m, D_MODEL), F32),
                   jax.ShapeDtypeStruct((m, HALF), I32),
                   jax.ShapeDtypeStruct((m, LANES), I32),
                   jax.ShapeDtypeStruct((m, LANES), F32),
                   jax.ShapeDtypeStruct((m, LANES), I32),
                   jax.ShapeDtypeStruct((1, LANES), F32)],
        compiler_params=_cparams(("arbitrary",)),
        name="outln_router",
    )(*xs, *ws, h2, g, b, rw3, rb, tri)


def _sc_mesh():
    return plsc.VectorSubcoreMesh(core_axis_name="core", subcore_axis_name="subcore")


def _pad_windows(idx2):
    return jnp.pad(idx2, ((0, 0), (0, LANES - SC_WIN)))


def _sc_scatter(x, pos):
    t, d = x.shape
    nw = t // SC_WIN
    idx = _pad_windows(pos.reshape(nw, SC_WIN, TOP_K).transpose(0, 2, 1).reshape(nw * TOP_K, SC_WIN))

    @pl.kernel(out_type=jax.ShapeDtypeStruct((t * TOP_K, d), x.dtype), mesh=_sc_mesh())
    def scatter_rows(x_hbm, i_hbm, o_hbm):
        def body(x_vmem, i_vmem):
            pltpu.sync_copy(x_vmem, o_hbm.at[i_vmem.at[0, pl.ds(0, SC_WIN)]])

        pltpu.emit_pipeline(
            body,
            grid=(nw * TOP_K,),
            in_specs=[pl.BlockSpec((SC_WIN, d), lambda i: (i // TOP_K, 0)),
                      pl.BlockSpec((1, LANES), lambda i: (i, 0))],
            out_specs=[],
            core_axis_name=("core", "subcore"),
            dimension_semantics=(pltpu.PARALLEL,),
        )(x_hbm, i_hbm)

    return scatter_rows(x, idx)


def _sc_gather(y, idx):
    n = idx.shape[0]
    d = y.shape[1]
    idx2 = _pad_windows(idx.reshape(n // SC_WIN, SC_WIN))

    @pl.kernel(out_type=jax.ShapeDtypeStruct((n, d), y.dtype), mesh=_sc_mesh())
    def gather_rows(y_hbm, i_hbm, o_hbm):
        def body(i_vmem, o_vmem):
            pltpu.sync_copy(y_hbm.at[i_vmem.at[0, pl.ds(0, SC_WIN)]], o_vmem)

        pltpu.emit_pipeline(
            body,
            grid=(n // SC_WIN,),
            in_specs=[pl.BlockSpec((1, LANES), lambda i: (i, 0))],
            out_specs=[pl.BlockSpec((SC_WIN, d), lambda i: (i, 0))],
            core_axis_name=("core", "subcore"),
            dimension_semantics=(pltpu.PARALLEL,),
        )(i_hbm, o_hbm)

    return gather_rows(y, idx2)


def _ffn_kernel(vb_ref, ve_ref, lo_ref, hi_ref, first_ref, efirst_ref, x_ref, wgu_ref, bgu_ref,
                wdn_ref, bdn_ref, y_ref, wgu_s, wdn_s):
    v = pl.program_id(0)
    lo = lo_ref[v]
    hi = hi_ref[v]

    @pl.when(efirst_ref[v] == 1)
    def _():
        for c in range(D_MODEL // FFN_FC):
            rs = slice(c * FFN_FC, (c + 1) * FFN_FC)
            wgu_s[rs, :] = wgu_ref[rs, :].astype(BF16)
            wdn_s[rs, :] = wdn_ref[rs, :].astype(BF16)

    @pl.when(hi > lo)
    def _():
        xw = x_ref[...]
        xl = _unpack_lo(xw).astype(BF16)
        xh = _unpack_hi(xw).astype(BF16)
        acc = jnp.zeros((x_ref.shape[0], D_MODEL), F32)
        for c in range(D_FF // FFN_FC):
            gs = slice(c * FFN_FC, (c + 1) * FFN_FC)
            us = slice(D_FF + c * FFN_FC, D_FF + (c + 1) * FFN_FC)
            gate = (jnp.dot(xl, wgu_s[:HALF, gs], preferred_element_type=F32)
                    + jnp.dot(xh, wgu_s[HALF:, gs], preferred_element_type=F32) + bgu_ref[:, gs])
            up = (jnp.dot(xl, wgu_s[:HALF, us], preferred_element_type=F32)
                  + jnp.dot(xh, wgu_s[HALF:, us], preferred_element_type=F32) + bgu_ref[:, us])
            gate = jnp.minimum(gate, SWIGLU_LIMIT)
            up = jnp.clip(up, -SWIGLU_LIMIT, SWIGLU_LIMIT)
            act = (up + 1.0) * (gate * jax.nn.sigmoid(gate * SWIGLU_ALPHA))
            acc = acc + jnp.dot(act.astype(BF16), wdn_s[gs, :], preferred_element_type=F32)
        new = _pack_halves(acc + bdn_ref[...])
        row = lax.broadcasted_iota(I32, new.shape, 0)
        keep = jnp.where(first_ref[v] == 1, jnp.zeros_like(new), y_ref[...])
        y_ref[...] = jnp.where(row >= lo, jnp.where(row < hi, new, keep), keep)


def _ffn(visits, xs, layer, wgu, bgu, wdn, bdn):
    n_rows, hw = xs.shape
    n_vis = visits[0].shape[0]
    blk = lambda v, vb, *_: (vb[v], 0)
    exp = lambda v, vb, ve, *_: (layer, ve[v], 0, 0)
    grid_spec = pltpu.PrefetchScalarGridSpec(
        num_scalar_prefetch=len(visits),
        grid=(n_vis,),
        in_specs=[pl.BlockSpec((FFN_TM, hw), blk),
                  pl.BlockSpec((None, None, D_MODEL, 2 * D_FF), exp),
                  pl.BlockSpec((None, None, 1, 2 * D_FF), exp),
                  pl.BlockSpec((None, None, D_FF, D_MODEL), exp),
                  pl.BlockSpec((None, None, 1, D_MODEL), exp)],
        out_specs=pl.BlockSpec((FFN_TM, hw), blk),
        scratch_shapes=[pltpu.VMEM((D_MODEL, 2 * D_FF), BF16), pltpu.VMEM((D_FF, D_MODEL), BF16)],
    )
    return pl.pallas_call(
        _ffn_kernel,
        grid_spec=grid_spec,
        out_shape=jax.ShapeDtypeStruct((n_rows, hw), I32),
        compiler_params=_cparams(("arbitrary",)),
        name="moe_ffn",
    )(*visits, xs, wgu, bgu, wdn, bdn)


def _combine_kernel(yg_ref, h_ref, gate_ref, g_ref, b_ref, o_ref):
    gate = gate_ref[...]
    h = h_ref[...]
    zl = DN_ALPHA * h[:, :HALF]
    zh = DN_ALPHA * h[:, HALF:]
    for k in range(TOP_K):
        w = yg_ref[k]
        gk = gate[:, k:k + 1]
        zl = zl + gk * _unpack_lo(w)
        zh = zh + gk * _unpack_hi(w)
    mu = (jnp.sum(zl, axis=-1, keepdims=True) + jnp.sum(zh, axis=-1, keepdims=True)) * (1.0 / D_MODEL)
    zl = zl - mu
    zh = zh - mu
    var = (jnp.sum(zl * zl, axis=-1, keepdims=True)
           + jnp.sum(zh * zh, axis=-1, keepdims=True)) * (1.0 / D_MODEL)
    r = lax.rsqrt(var + LN_EPS)
    o_ref[:, :HALF] = zl * r * g_ref[:, :HALF] + b_ref[:, :HALF]
    o_ref[:, HALF:] = zh * r * g_ref[:, HALF:] + b_ref[:, HALF:]


def _combine(yg, h1, gates, g, b):
    m, d = h1.shape
    tm = min(COMB_TM, m)
    return pl.pallas_call(
        _combine_kernel,
        grid=(m // tm,),
        in_specs=[pl.BlockSpec((TOP_K, tm, HALF), lambda i: (0, i, 0)),
                  pl.BlockSpec((tm, d), lambda i: (i, 0)),
                  pl.BlockSpec((tm, LANES), lambda i: (i, 0)),
                  _full(g, 1), _full(b, 1)],
        out_specs=pl.BlockSpec((tm, d), lambda i: (i, 0)),
        out_shape=jax.ShapeDtypeStruct((m, d), F32),
        compiler_params=_cparams(("parallel",)),
        name="moe_combine",
    )(yg, h1, gates, g, b)


def _routing_tables(top_e, rank, counts):
    n_tok = top_e.shape[0]
    n_blocks = n_tok * TOP_K // FFN_TM
    n_vis = n_blocks + N_EXPERTS - 1
    end = jnp.cumsum(counts)
    start = end - counts
    onehot = top_e[:, :, None] == jnp.arange(N_EXPERTS, dtype=I32)[None, None, :]
    pos = (rank + jnp.sum(jnp.where(onehot, start[None, None, :], 0), axis=-1)).astype(I32)

    first = start // FFN_TM
    last = jnp.maximum(end - 1, 0) // FFN_TM
    nvis = jnp.where(counts > 0, last - first + 1, 0)
    vend = jnp.cumsum(nvis)
    vstart = vend - nvis
    total = vend[-1]
    v = jnp.arange(n_vis, dtype=I32)
    valid = v < total
    ev = jnp.sum((vend[None, :] <= jnp.minimum(v, total - 1)[:, None]).astype(I32), axis=1)
    ev = jnp.minimum(ev, N_EXPERTS - 1)
    pick = lambda tbl: jnp.sum(jnp.where(ev[:, None] == jnp.arange(N_EXPERTS, dtype=I32)[None, :],
                                         tbl[None, :], 0), axis=1)
    bv = jnp.where(valid, pick(first) + v - pick(vstart), n_blocks - 1).astype(I32)
    lo = jnp.where(valid, jnp.maximum(pick(start), bv * FFN_TM) - bv * FFN_TM, 0).astype(I32)
    hi = jnp.where(valid, jnp.minimum(pick(end), (bv + 1) * FFN_TM) - bv * FFN_TM, 0).astype(I32)
    prev_b = jnp.concatenate([jnp.full((1,), -1, I32), bv[:-1]])
    prev_e = jnp.concatenate([jnp.full((1,), -1, I32), ev[:-1]])
    fi = jnp.where(valid & (bv != prev_b), 1, 0).astype(I32)
    efi = jnp.where(ev != prev_e, 1, 0).astype(I32)
    return pos, (bv, ev.astype(I32), lo, hi, fi, efi)


def _moe(routed, layer, wgu, bgu, wdn, bdn, g, b):
    h1, h1p, te, gates, rk, cnt = routed
    n_tok = h1.shape[0]
    pos, visits = _routing_tables(te[:, :TOP_K], rk[:, :TOP_K], cnt[0, :N_EXPERTS].astype(I32))
    xs = _sc_scatter(h1p, pos)
    y = _ffn(visits, xs, layer, wgu, bgu, wdn, bdn)
    yg = _sc_gather(y, jnp.transpose(pos).reshape(-1)).reshape(TOP_K, n_tok, HALF)
    return _combine(yg, h1, gates, g, b)


def _row(a):
    return a.reshape(1, -1).astype(F32)


def _even_layer(h2, bsz, seq, w_in, q_norm, kv_norm, w_uq, w_uk, w_uv, w_iq, idx_g, idx_b, v_g, v_b,
                w_s, b_s, w_out, ln_g, ln_b, rw, rb):
    o4 = D_CQ + D_C + D_IDX + H_IDX
    w_in_p = jnp.concatenate(
        [w_in[:, :o4], jnp.zeros((D_MODEL, E_IN_PAD - E_IN_EVEN), F32), w_in[:, o4:]], axis=1)
    pad_idx = lambda a: jnp.pad(_row(a), ((0, 0), (0, LANES - D_IDX)))
    cq, ckv, kidx, widx, uv = _proj_even(h2, w_in_p.astype(BF16), _row(q_norm), _row(kv_norm),
                                         pad_idx(idx_g), pad_idx(idx_b))
    as3 = lambda a: a.reshape(bsz, seq, a.shape[-1])

    n_sel = min(TOPK_MAX, seq // 4)
    tq = min(DSA_TQ, seq)
    wuq = jnp.transpose(w_uq, (1, 0, 2)).astype(BF16)
    wiq = jnp.pad(jnp.transpose(w_iq, (1, 0, 2)),
                  ((0, 0), (0, 0), (0, LANES - D_IDX))).astype(BF16)
    wuk = w_uk.astype(BF16)
    eye = jnp.eye(H_A, dtype=F32)
    wuv = (w_uv[:, :, None, :] * eye[:, None, :, None]).reshape(H_A, D_C, H_A * DH_A).astype(BF16)
    o_a = jnp.concatenate(
        [_dsa_tile(as3(cq), as3(ckv), as3(kidx), as3(widx), qt, tq, n_sel, wuq, wuk, wiq, wuv)
         for qt in range(seq // tq)], axis=1)

    gbias = jnp.repeat(jnp.transpose(b_s), CG_B, axis=1)
    o_b = _gmlp(as3(uv), _row(v_g), _row(v_b), w_s, gbias)

    n_a = H_A * DH_A
    w_out_b = w_out.astype(BF16)
    return _outln([o_a.reshape(bsz * seq, n_a), o_b.reshape(bsz * seq, GMLP_WIDTH)],
                  [w_out_b[:n_a], w_out_b[n_a:]], h2, _row(ln_g), _row(ln_b), rw, rb)


def _odd_layer(h2, bsz, seq, layer, w_in, lq1, lk1, lq2, lk2, subln_g, w_out, ln_g, ln_b, rw, rb):
    lam_init = 0.8 - 0.6 * math.exp(-0.3 * layer)
    proj3 = _proj_odd(h2, w_in.astype(BF16)).reshape(bsz, seq, 3 * QK_W_C)
    tq = min(DIFF_TQ, seq)
    kpos = jnp.stack([_alibi_key_cols(seq, DH_C), _alibi_key_cols(seq, 0)])
    qa0 = _alibi_query_cols(H_C)
    qa = jnp.stack([jnp.roll(qa0, DH_C, axis=-1), qa0], axis=1)
    diag = _alibi_diag(H_C, tq)
    o = jnp.concatenate(
        [_diff_tile(proj3, qt, tq, lam_init, kpos, qa, diag, _row(lq1), _row(lk1), _row(lq2), _row(lk2),
                    _row(subln_g)) for qt in range(seq // tq)], axis=1)
    return _outln([o.reshape(bsz * seq, QK_W_C)], [w_out.astype(BF16)], h2, _row(ln_g), _row(ln_b),
                  rw, rb)


def kernel(x, ev_w_in, ev_q_norm, ev_kv_norm, ev_w_uq, ev_w_uk, ev_w_uv, ev_w_iq, ev_idx_k_g, ev_idx_k_b, ev_v_norm_g, ev_v_norm_b, ev_w_s, ev_b_s, ev_w_out, od_w_in, od_lambda_q1, od_lambda_k1, od_lambda_q2, od_lambda_k2, od_subln_g, od_w_out, ln1_g, ln1_b, ln2_g, ln2_b, router_w, router_b, exp_w_gu, exp_b_gu, exp_w_dn, exp_b_dn):
    bsz, seq, d = x.shape
    h2 = x.reshape(bsz * seq, d)
    for l in range(DEPTH):
        j = l // 2
        rw = jnp.pad(router_w[l], ((0, 0), (0, LANES - N_EXPERTS)))
        rb = jnp.pad(router_b[l], (0, LANES - N_EXPERTS), constant_values=-jnp.inf).reshape(1, LANES)
        if l % 2 == 0:
            routed = _even_layer(h2, bsz, seq, ev_w_in[j], ev_q_norm[j], ev_kv_norm[j],
                                 ev_w_uq[j], ev_w_uk[j], ev_w_uv[j], ev_w_iq[j], ev_idx_k_g[j],
                                 ev_idx_k_b[j], ev_v_norm_g[j], ev_v_norm_b[j], ev_w_s[j],
                                 ev_b_s[j], ev_w_out[j], ln1_g[l], ln1_b[l], rw, rb)
        else:
            routed = _odd_layer(h2, bsz, seq, l, od_w_in[j], od_lambda_q1[j], od_lambda_k1[j],
                                od_lambda_q2[j], od_lambda_k2[j], od_subln_g[j], od_w_out[j],
                                ln1_g[l], ln1_b[l], rw, rb)
        h2 = _moe(routed, l, exp_w_gu, exp_b_gu[:, :, None, :], exp_w_dn, exp_b_dn[:, :, None, :],
                  _row(ln2_g[l]), _row(ln2_b[l]))
    return h2.reshape(bsz, seq, d)
```

```python
import functools
import math

import jax
import jax.numpy as jnp
from jax import lax
from jax.experimental import pallas as pl
from jax.experimental.pallas import tpu as pltpu
from jax.experimental.pallas import tpu_sc as plsc

F32 = jnp.float32
BF16 = jnp.bfloat16
I32 = jnp.int32

D_MODEL = 1024
DEPTH = 4
CHUNK = 64
CHUNK_SHIFT = 6
H_A = 8
DH_A = 64
D_CQ = 256
D_C = 128
H_IDX = 4
D_IDX = 64
TOPK_MAX = 256
GMLP_CHUNK = 128
G_B = 8
GMLP_WIDTH = 512
CG_B = GMLP_WIDTH // G_B
H_C = 8
DH_C = 64
QK_W_C = H_C * 2 * DH_C
N_EXPERTS = 32
TOP_K = 4
D_FF = 1024
SWIGLU_ALPHA = 1.702
SWIGLU_LIMIT = 7.0
DN_ALPHA = (2 * DEPTH) ** 0.25
E_IN_EVEN = D_CQ + D_C + D_IDX + H_IDX + 2 * GMLP_WIDTH
LN_EPS = 1e-5
NEG = -1e30
LOG2E = 1.4426950408889634

LANES = 128
E_IN_PAD = 1536
INT_MIN = -(2 ** 31)
HALF = D_MODEL // 2
HI_MASK = -65536
SC_WIN = 64
POS_RADIX = 256
N_POS_COLS = 6

PROJ_TM = 512
DSA_TQ = 256
DIFF_TQ = 512
GMLP_TG = 512
FFN_TM = 512
FFN_FC = 512
COMB_TM = 512
VMEM_LIMIT = 56 * 1024 * 1024


def _cparams(sem):
    return pltpu.CompilerParams(dimension_semantics=sem, vmem_limit_bytes=VMEM_LIMIT)


def _full(a, n_grid):
    zeros = (0,) * a.ndim
    return pl.BlockSpec(a.shape, lambda *_: zeros)


def _alibi_slopes_l2(n):
    return jnp.exp2(-8.0 * jnp.arange(1, n + 1, dtype=F32) / n) * LOG2E


def _alibi_query_cols(n):
    s = _alibi_slopes_l2(n)
    hi = s.astype(BF16).astype(F32)
    mid = (s - hi).astype(BF16).astype(F32)
    lo = (s - hi - mid).astype(BF16).astype(F32)
    cols = jnp.stack([hi * POS_RADIX, hi, mid * POS_RADIX, mid, lo * POS_RADIX, lo], axis=-1)
    return jnp.pad(cols, ((0, 0), (0, LANES - N_POS_COLS))).reshape(n, 1, LANES)


def _alibi_key_cols(seq, lane0):
    s = jnp.arange(seq, dtype=I32)
    a = (s // POS_RADIX).astype(F32)
    b = (s % POS_RADIX).astype(F32)
    cols = jnp.stack([a, b, a, b, a, b], axis=-1)
    return jnp.pad(cols, ((0, 0), (lane0, LANES - N_POS_COLS - lane0))).astype(BF16)


def _alibi_diag(n, tq):
    t = jnp.arange(tq, dtype=I32)[:, None]
    s = jnp.arange(tq, dtype=I32)[None, :]
    ok = (s >> CHUNK_SHIFT) <= (t >> CHUNK_SHIFT)
    ahead = jnp.maximum(s - t, 0).astype(F32)
    corr = -2.0 * _alibi_slopes_l2(n)[:, None, None] * ahead[None]
    return jnp.where(ok[None], corr, NEG)


def _proj_even_kernel(x_ref, w_ref, qg_ref, kvg_ref, ig_ref, ib_ref,
                      cq_ref, ckv_ref, kidx_ref, widx_ref, uv_ref):
    acc = jnp.dot(x_ref[...].astype(BF16), w_ref[...], preferred_element_type=F32)
    cq = acc[:, :D_CQ]
    cq_ref[...] = (cq * lax.rsqrt(jnp.mean(cq * cq, axis=-1, keepdims=True) + LN_EPS)
                   * qg_ref[...]).astype(cq_ref.dtype)
    kv = acc[:, D_CQ:D_CQ + D_C]
    ckv_ref[...] = (kv * lax.rsqrt(jnp.mean(kv * kv, axis=-1, keepdims=True) + LN_EPS)
                    * kvg_ref[...]).astype(ckv_ref.dtype)
    blk = acc[:, D_CQ + D_C:D_CQ + D_C + LANES]
    lane = lax.broadcasted_iota(I32, blk.shape, 1)
    is_k = lane < D_IDX
    mu = jnp.sum(jnp.where(is_k, blk, 0.0), axis=-1, keepdims=True) * (1.0 / D_IDX)
    kc = jnp.where(is_k, blk - mu, 0.0)
    var = jnp.sum(kc * kc, axis=-1, keepdims=True) * (1.0 / D_IDX)
    kidx_ref[...] = jnp.where(is_k, kc * lax.rsqrt(var + LN_EPS) * ig_ref[...] + ib_ref[...],
                              0.0).astype(kidx_ref.dtype)
    widx_ref[...] = jnp.where(is_k, 0.0, blk * (H_IDX ** -0.5))
    uv_ref[...] = acc[:, E_IN_PAD - 2 * GMLP_WIDTH:]


def _proj_even(x2, w, qg, kvg, ig, ib):
    m, k = x2.shape
    tm = min(PROJ_TM, m)
    row = lambda n: pl.BlockSpec((tm, n), lambda i: (i, 0))
    consts = (qg, kvg, ig, ib)
    return pl.pallas_call(
        _proj_even_kernel,
        grid=(m // tm,),
        in_specs=[row(k), _full(w, 1)] + [_full(c, 1) for c in consts],
        out_specs=[row(D_CQ), row(D_C), row(LANES), row(LANES), row(2 * GMLP_WIDTH)],
        out_shape=[jax.ShapeDtypeStruct((m, D_CQ), BF16),
                   jax.ShapeDtypeStruct((m, D_C), BF16),
                   jax.ShapeDtypeStruct((m, LANES), BF16),
                   jax.ShapeDtypeStruct((m, LANES), F32),
                   jax.ShapeDtypeStruct((m, 2 * GMLP_WIDTH), F32)],
        compiler_params=_cparams(("parallel",)),
        name="proj_even",
    )(x2, w, *consts)


def _proj_odd_kernel(x_ref, w_ref, qk_ref, vt_ref):
    acc = jnp.dot(x_ref[...].astype(BF16), w_ref[...], preferred_element_type=F32)
    qk_ref[:, :QK_W_C] = (acc[:, :QK_W_C] * (DH_C ** -0.5 * LOG2E)).astype(qk_ref.dtype)
    qk_ref[:, QK_W_C:] = acc[:, QK_W_C:2 * QK_W_C].astype(qk_ref.dtype)
    w = 2 * DH_C
    for h in range(H_C):
        v = acc[:, 2 * QK_W_C + h * w:2 * QK_W_C + (h + 1) * w]
        vt_ref[h * w:(h + 1) * w, :] = v.T.astype(vt_ref.dtype)


def _proj_odd(x2, w, seq):
    m, k = x2.shape
    tm = min(PROJ_TM, seq)
    per_seq = seq // tm
    return pl.pallas_call(
        _proj_odd_kernel,
        grid=(m // tm,),
        in_specs=[pl.BlockSpec((tm, k), lambda i: (i, 0)), _full(w, 1)],
        out_specs=[pl.BlockSpec((tm, 2 * QK_W_C), lambda i: (i, 0)),
                   pl.BlockSpec((None, QK_W_C, tm), lambda i: (i // per_seq, 0, i % per_seq))],
        out_shape=[jax.ShapeDtypeStruct((m, 2 * QK_W_C), BF16),
                   jax.ShapeDtypeStruct((m // seq, QK_W_C, seq), BF16)],
        compiler_params=_cparams(("parallel",)),
        name="proj_odd",
    )(x2, w)


_NT = (((1,), (1,)), ((), ()))


COL_GROUPS = 8


def _col_reduce(x, red, comb):
    step = x.shape[0] // COL_GROUPS
    vals = [red(x[i * step:(i + 1) * step], axis=0, keepdims=True) for i in range(COL_GROUPS)]
    while len(vals) > 1:
        vals = [comb(vals[i], vals[i + 1]) for i in range(0, len(vals), 2)]
    return vals[0]


def _softmax_pv_t(parts, vt_parts):
    m = None
    for lg in parts:
        mx = _col_reduce(lg, jnp.max, jnp.maximum)
        m = mx if m is None else jnp.maximum(m, mx)
    l = None
    o = None
    for lg, vt in zip(parts, vt_parts):
        p = jnp.exp2(lg - m)
        s = _col_reduce(p, jnp.sum, jnp.add)
        d = jnp.dot(vt, p.astype(BF16), preferred_element_type=F32)
        l = s if l is None else l + s
        o = d if o is None else o + d
    return o * (1.0 / l)


def _softmax_pv(parts, kv_parts):
    m = None
    for lg in parts:
        mx = jnp.max(lg, axis=-1, keepdims=True)
        m = mx if m is None else jnp.maximum(m, mx)
    l = None
    o = None
    for lg, kv in zip(parts, kv_parts):
        p = jnp.exp2(lg - m)
        s = jnp.sum(p, axis=-1, keepdims=True)
        d = jnp.dot(p.astype(BF16), kv, preferred_element_type=F32)
        l = s if l is None else l + s
        o = d if o is None else o + d
    return o * (1.0 / l)


def _dsa_kernel(cq_ref, ckv_ref, kidx_ref, widx_ref, wuq_ref, wuk_ref, wiq_ref, wuv_ref, srow_ref,
                diag_ref, o_ref, bias_ref, *, q0, n_sel):
    tq = cq_ref.shape[0]
    sk = ckv_ref.shape[0]
    cqb = cq_ref[...]
    kidx = kidx_ref[...]
    widx = widx_ref[...]

    score = None
    for h in range(H_IDX):
        qi = jnp.dot(cqb, wiq_ref[h], preferred_element_type=F32).astype(BF16)
        r = lax.dot_general(qi, kidx, _NT, preferred_element_type=F32)
        term = jnp.maximum(r * (D_IDX ** -0.5), 0.0) * widx[:, D_IDX + h:D_IDX + h + 1]
        score = term if score is None else score + term
    t_pos = q0 + lax.broadcasted_iota(I32, (tq, sk), 0)
    s_pos = lax.broadcasted_iota(I32, (tq, sk), 1)
    allowed = (s_pos >> CHUNK_SHIFT) <= (t_pos >> CHUNK_SHIFT)
    score = jnp.where(allowed, score, NEG)

    bits = lax.bitcast_convert_type(score, I32)
    key = jnp.where(bits < 0, bits ^ 0x7FFFFFFF, bits)
    key = jnp.where(key == -1, 0, key)

    def count(mask):
        return jnp.sum(jnp.where(mask, 1.0, 0.0), axis=-1, keepdims=True)

    nsel = float(n_sel)
    cur = jnp.where(count(key >= 0) >= nsel, 0, INT_MIN).astype(I32)

    def search(i, cur):
        cand = cur | lax.shift_left(jnp.int32(1), 30 - i)
        return jnp.where(count(key >= cand) >= nsel, cand, cur)

    cur = lax.fori_loop(0, 31, search, cur)

    need = nsel - count(key > cur)
    ri = lax.broadcasted_iota(I32, (LANES, LANES), 0)
    ci = lax.broadcasted_iota(I32, (LANES, LANES), 1)
    tri = jnp.where(ri < ci, 1.0, 0.0).astype(BF16)
    ones_b = jnp.ones((LANES, LANES), BF16)
    off = jnp.zeros((tq, 1), F32)
    for c in range(sk // LANES):
        sl = slice(c * LANES, (c + 1) * LANES)
        keyc = key[:, sl]
        eqc = jnp.where(keyc == cur, 1.0, 0.0)
        eqb = eqc.astype(BF16)
        rank = jnp.dot(eqb, tri, preferred_element_type=F32) + off
        sel = jnp.where(keyc > cur, 1.0, jnp.where(rank < need, eqc, 0.0))
        bias_ref[:, sl] = jnp.where(sel > 0.5, 0.0, NEG)
        off = off + jnp.dot(eqb, ones_b, preferred_element_type=F32)[:, :1]

    n_lo = sk - tq
    kv_parts = ([ckv_ref[:n_lo, :]] if n_lo else []) + [ckv_ref[n_lo:, :]]
    acc = jnp.zeros((tq, H_A * DH_A), F32)
    for h in range(H_A):
        qh = jnp.dot(cqb, wuq_ref[h], preferred_element_type=F32).astype(BF16)
        ql = (jnp.dot(qh, wuk_ref[h], preferred_element_type=F32)
              * (DH_A ** -0.5 * LOG2E)).astype(BF16)
        parts = []
        if n_lo:
            lg = lax.dot_general(ql, kv_parts[0], _NT, preferred_element_type=F32)
            parts.append(lg + bias_ref[:, :n_lo] + srow_ref[h][:, :n_lo])
        lg = lax.dot_general(ql, kv_parts[-1], _NT, preferred_element_type=F32)
        parts.append(lg + bias_ref[:, n_lo:] + diag_ref[h])
        ol = _softmax_pv(parts, kv_parts)
        acc = acc + jnp.dot(ol.astype(BF16), wuv_ref[h], preferred_element_type=F32)
    o_ref[...] = acc.astype(o_ref.dtype)


def _dsa_tile(cq3, ckv3, kidx3, widx3, qt, tq, n_sel, wuq, wuk, wiq, wuv):
    b, s, _ = cq3.shape
    sk = (qt + 1) * tq
    slopes = _alibi_slopes_l2(H_A)
    srow = (slopes[:, None] * jnp.arange(sk, dtype=F32)[None, :]).reshape(H_A, 1, sk)
    diag = _alibi_diag(H_A, tq) + srow[:, :, sk - tq:]
    consts = (wuq, wuk, wiq, wuv, srow, diag)
    return pl.pallas_call(
        functools.partial(_dsa_kernel, q0=qt * tq, n_sel=n_sel),
        grid=(b,),
        in_specs=[pl.BlockSpec((None, tq, D_CQ), lambda i: (i, qt, 0)),
                  pl.BlockSpec((None, sk, D_C), lambda i: (i, 0, 0)),
                  pl.BlockSpec((None, sk, LANES), lambda i: (i, 0, 0)),
                  pl.BlockSpec((None, tq, LANES), lambda i: (i, qt, 0))]
        + [_full(c, 1) for c in consts],
        out_specs=pl.BlockSpec((None, tq, H_A * DH_A), lambda i: (i, 0, 0)),
        out_shape=jax.ShapeDtypeStruct((b, tq, H_A * DH_A), BF16),
        scratch_shapes=[pltpu.VMEM((tq, sk), F32)],
        compiler_params=_cparams(("parallel",)),
        name=f"dsa_q{qt}",
    )(cq3, ckv3, kidx3, widx3, *consts)


def _gelu(x):
    return 0.5 * x * (1.0 + jnp.tanh(0.7978845608028654 * (x + 0.044715 * (x * x * x))))


def _gmlp_kernel(u_ref, v_ref, vg_ref, vb_ref, ws_ref, bias_ref, o_ref):
    tg = u_ref.shape[0]
    ri = lax.broadcasted_iota(I32, (GMLP_CHUNK, GMLP_CHUNK), 0)
    ci = lax.broadcasted_iota(I32, (GMLP_CHUNK, GMLP_CHUNK), 1)
    tril = ri >= ci
    grp = lax.broadcasted_iota(I32, (GMLP_CHUNK, GMLP_WIDTH), 1) // CG_B
    ws = [jnp.where(tril, ws_ref[g], 0.0).astype(BF16) for g in range(G_B)]
    for c in range(tg // GMLP_CHUNK):
        sl = slice(c * GMLP_CHUNK, (c + 1) * GMLP_CHUNK)
        u = _gelu(u_ref[sl, :])
        v = _gelu(v_ref[sl, :])
        mu = jnp.mean(v, axis=-1, keepdims=True)
        vc = v - mu
        var = jnp.mean(vc * vc, axis=-1, keepdims=True)
        vn = (vc * lax.rsqrt(var + LN_EPS) * vg_ref[...] + vb_ref[...]).astype(BF16)
        mixed = bias_ref[...]
        for g in range(G_B):
            r = jnp.dot(ws[g], vn, preferred_element_type=F32)
            mixed = mixed + jnp.where(grp == g, r, 0.0)
        o_ref[sl, :] = (u * mixed).astype(o_ref.dtype)


def _gmlp(uv3, vg, vb, ws, bias):
    b, s, _ = uv3.shape
    tg = min(GMLP_TG, s)
    consts = (vg, vb, ws, bias)
    return pl.pallas_call(
        _gmlp_kernel,
        grid=(b, s // tg),
        in_specs=[pl.BlockSpec((None, tg, GMLP_WIDTH), lambda i, j: (i, j, 0)),
                  pl.BlockSpec((None, tg, GMLP_WIDTH), lambda i, j: (i, j, 1))]
        + [_full(c, 2) for c in consts],
        out_specs=pl.BlockSpec((None, tg, GMLP_WIDTH), lambda i, j: (i, j, 0)),
        out_shape=jax.ShapeDtypeStruct((b, s, GMLP_WIDTH), BF16),
        compiler_params=_cparams(("parallel", "parallel")),
        name="gmlp",
    )(uv3, uv3, *consts)


def _diff_kernel(q_ref, k_ref, vt_ref, kpos_ref, qa_ref, diag_ref, lq1_ref, lk1_ref, lq2_ref, lk2_ref,
                 g_ref, o_ref, *, lam_init):
    tq = q_ref.shape[0]
    sk = k_ref.shape[0]
    n_lo = sk - tq
    lam = (jnp.exp(jnp.sum(lq1_ref[...] * lk1_ref[...], axis=-1, keepdims=True))
           - jnp.exp(jnp.sum(lq2_ref[...] * lk2_ref[...], axis=-1, keepdims=True)) + lam_init)
    q = q_ref[...]
    k = k_ref[...]
    vt_parts = ([vt_ref[:, :n_lo]] if n_lo else []) + [vt_ref[:, n_lo:]]
    q_lane = lax.broadcasted_iota(I32, q.shape, 1)
    k_lane = lax.broadcasted_iota(I32, k.shape, 1)
    outs = []
    for m in range(2):
        q_own = (q_lane < DH_C) if m == 0 else (q_lane >= DH_C)
        k_own = (k_lane < DH_C) if m == 0 else (k_lane >= DH_C)
        qm = jnp.where(q_own, q, jnp.broadcast_to(qa_ref[m], q.shape).astype(BF16))
        km = jnp.where(k_own, k, kpos_ref[m])
        parts = []
        if n_lo:
            parts.append(lax.dot_general(km[:n_lo], qm, _NT, preferred_element_type=F32))
        lg = lax.dot_general(km[n_lo:], qm, _NT, preferred_element_type=F32)
        parts.append(lg + diag_ref[...])
        outs.append(_softmax_pv_t(parts, vt_parts))
    o = outs[0] - lam * outs[1]
    o = o * lax.rsqrt(jnp.mean(o * o, axis=0, keepdims=True) + LN_EPS) * g_ref[...]
    o_ref[...] = (o * (1.0 - lam_init)).T.astype(o_ref.dtype)


def _diff_tile(qk3, vt3, qt, tq, lam_init, kpos, qa, diag, lq1, lk1, lq2, lk2, g):
    b, s, _ = qk3.shape
    sk = (qt + 1) * tq
    w = 2 * DH_C
    consts = (lq1, lk1, lq2, lk2, g)
    return pl.pallas_call(
        functools.partial(_diff_kernel, lam_init=lam_init),
        grid=(b, H_C),
        in_specs=[pl.BlockSpec((None, tq, w), lambda i, j: (i, qt, j)),
                  pl.BlockSpec((None, sk, w), lambda i, j: (i, 0, H_C + j)),
                  pl.BlockSpec((None, w, sk), lambda i, j: (i, j, 0)),
                  pl.BlockSpec((2, sk, w), lambda i, j: (0, 0, 0)),
                  pl.BlockSpec((None, 2, 1, w), lambda i, j: (j, 0, 0, 0)),
                  pl.BlockSpec((None, tq, tq), lambda i, j: (j, 0, 0))]
        + [_full(c, 2) for c in consts],
        out_specs=pl.BlockSpec((None, tq, w), lambda i, j: (i, 0, j)),
        out_shape=jax.ShapeDtypeStruct((b, tq, H_C * w), BF16),
        compiler_params=_cparams(("parallel", "parallel")),
        name=f"diff_q{qt}",
    )(qk3, qk3, vt3, kpos, qa, diag, *consts)


def _pack_halves(x):
    lo = lax.bitcast_convert_type(x[:, :HALF].astype(BF16).astype(F32), I32)
    hi = lax.bitcast_convert_type(x[:, HALF:].astype(BF16).astype(F32), I32)
    return lax.shift_right_logical(lo, 16) | (hi & HI_MASK)


def _unpack_lo(w):
    return lax.bitcast_convert_type(lax.shift_left(w, 16), F32)


def _unpack_hi(w):
    return lax.bitcast_convert_type(w & HI_MASK, F32)


def _layer_norm_rows(z, g, b):
    mu = jnp.mean(z, axis=-1, keepdims=True)
    zc = z - mu
    var = jnp.mean(zc * zc, axis=-1, keepdims=True)
    return zc * lax.rsqrt(var + LN_EPS) * g + b


def _split2(x):
    hi = x.astype(BF16)
    mid = (x - hi.astype(F32)).astype(BF16)
    return hi, mid


def _outln_kernel(*refs, n_in):
    xs = refs[:n_in]
    ws = refs[n_in:2 * n_in]
    (h_ref, g_ref, b_ref, rw_ref, rb_ref, tri_ref,
     h1_ref, hp_ref, te_ref, tg_ref, rk_ref, cnt_ref) = refs[2 * n_in:]
    acc = None
    for x_ref, w_ref in zip(xs, ws):
        d = jnp.dot(x_ref[...], w_ref[...], preferred_element_type=F32)
        acc = d if acc is None else acc + d
    h1 = _layer_norm_rows(DN_ALPHA * h_ref[...] + acc, g_ref[...], b_ref[...])
    h1_ref[...] = h1
    hp_ref[...] = _pack_halves(h1)

    h_hi, h_mid = _split2(h1)
    dotf = lambda a, b: jnp.dot(a, b, preferred_element_type=F32)
    logits = dotf(h_hi, rw_ref[0]) + dotf(h_mid, rw_ref[0]) + dotf(h_hi, rw_ref[1]) + rb_ref[...]
    lane = lax.broadcasted_iota(I32, logits.shape, 1)
    vals, idxs = [], []
    for _ in range(TOP_K):
        mx = jnp.max(logits, axis=-1, keepdims=True)
        ix = jnp.min(jnp.where(logits == mx, lane, LANES), axis=-1, keepdims=True)
        vals.append(mx)
        idxs.append(ix)
        logits = jnp.where(lane == ix, -jnp.inf, logits)
    es = [jnp.exp(v - vals[0]) for v in vals]
    den = es[0] + es[1] + es[2] + es[3]

    @pl.when(pl.program_id(0) == 0)
    def _():
        cnt_ref[...] = jnp.zeros_like(cnt_ref)

    member = jnp.zeros(lane.shape, F32)
    for k in range(TOP_K):
        member = jnp.where(lane == idxs[k], 1.0, member)
    before = jnp.dot(tri_ref[...], member.astype(BF16), preferred_element_type=F32) + cnt_ref[...]
    te = jnp.zeros(lane.shape, I32)
    tg = jnp.zeros(lane.shape, F32)
    rk = jnp.zeros(lane.shape, F32)
    for k in range(TOP_K):
        rank_k = jnp.sum(jnp.where(lane == idxs[k], before, 0.0), axis=-1, keepdims=True)
        te = jnp.where(lane == k, idxs[k], te)
        tg = jnp.where(lane == k, es[k] / den, tg)
        rk = jnp.where(lane == k, rank_k, rk)
    te_ref[...] = te
    tg_ref[...] = tg
    rk_ref[...] = rk.astype(I32)
    cnt_ref[...] += jnp.sum(member, axis=0, keepdims=True)


def _outln(xs, ws, h2, g, b, rw, rb):
    m = h2.shape[0]
    tm = min(PROJ_TM, m)
    n_in = len(xs)
    row = lambda a: pl.BlockSpec((tm, a.shape[1]), lambda i: (i, 0))
    ri = lax.broadcasted_iota(I32, (tm, tm), 0)
    ci = lax.broadcasted_iota(I32, (tm, tm), 1)
    tri = (ci < ri).astype(BF16)
    rw3 = jnp.stack(_split2(rw))
    return pl.pallas_call(
        functools.partial(_outln_kernel, n_in=n_in),
        grid=(m // tm,),
        in_specs=[row(x) for x in xs] + [_full(w, 1) for w in ws]
        + [row(h2), _full(g, 1), _full(b, 1), _full(rw3, 1), _full(rb, 1), _full(tri, 1)],
        out_specs=[pl.BlockSpec((tm, D_MODEL), lambda i: (i, 0)),
                   pl.BlockSpec((tm, HALF), lambda i: (i, 0)),
                   pl.BlockSpec((tm, LANES), lambda i: (i, 0)),
                   pl.BlockSpec((tm, LANES), lambda i: (i, 0)),
                   pl.BlockSpec((tm, LANES), lambda i: (i, 0)),
                   pl.BlockSpec((1, LANES), lambda i: (0, 0))],
        out_shape=[jax.ShapeDtypeStruct((m, D_MODEL), F32),
                   jax.ShapeDtypeStruct((m, HALF), I32),
                   jax.ShapeDtypeStruct((m, LANES), I32),
                   jax.ShapeDtypeStruct((m, LANES), F32),
                   jax.ShapeDtypeStruct((m, LANES), I32),
                   jax.ShapeDtypeStruct((1, LANES), F32)],
        compiler_params=_cparams(("arbitrary",)),
        name="outln_router",
    )(*xs, *ws, h2, g, b, rw3, rb, tri)


def _sc_mesh():
    return plsc.VectorSubcoreMesh(core_axis_name="core", subcore_axis_name="subcore")


def _pad_windows(idx2):
    return jnp.pad(idx2, ((0, 0), (0, LANES - SC_WIN)))


def _sc_scatter(x, pos):
    t, d = x.shape
    nw = t // SC_WIN
    idx = _pad_windows(pos.reshape(nw, SC_WIN, TOP_K).transpose(0, 2, 1).reshape(nw * TOP_K, SC_WIN))

    @pl.kernel(out_type=jax.ShapeDtypeStruct((t * TOP_K, d), x.dtype), mesh=_sc_mesh())
    def scatter_rows(x_hbm, i_hbm, o_hbm):
        def body(x_vmem, i_vmem):
            pltpu.sync_copy(x_vmem, o_hbm.at[i_vmem.at[0, pl.ds(0, SC_WIN)]])

        pltpu.emit_pipeline(
            body,
            grid=(nw * TOP_K,),
            in_specs=[pl.BlockSpec((SC_WIN, d), lambda i: (i // TOP_K, 0)),
                      pl.BlockSpec((1, LANES), lambda i: (i, 0))],
            out_specs=[],
            core_axis_name=("core", "subcore"),
            dimension_semantics=(pltpu.PARALLEL,),
        )(x_hbm, i_hbm)

    return scatter_rows(x, idx)


def _sc_gather(y, idx):
    n = idx.shape[0]
    d = y.shape[1]
    idx2 = _pad_windows(idx.reshape(n // SC_WIN, SC_WIN))

    @pl.kernel(out_type=jax.ShapeDtypeStruct((n, d), y.dtype), mesh=_sc_mesh())
    def gather_rows(y_hbm, i_hbm, o_hbm):
        def body(i_vmem, o_vmem):
            pltpu.sync_copy(y_hbm.at[i_vmem.at[0, pl.ds(0, SC_WIN)]], o_vmem)

        pltpu.emit_pipeline(
            body,
            grid=(n // SC_WIN,),
            in_specs=[pl.BlockSpec((1, LANES), lambda i: (i, 0))],
            out_specs=[pl.BlockSpec((SC_WIN, d), lambda i: (i, 0))],
            core_axis_name=("core", "subcore"),
            dimension_semantics=(pltpu.PARALLEL,),
        )(i_hbm, o_hbm)

    return gather_rows(y, idx2)


def _ffn_kernel(vb_ref, ve_ref, lo_ref, hi_ref, first_ref, efirst_ref, x_ref, wgu_ref, bgu_ref,
                wdn_ref, bdn_ref, y_ref, wgu_s, wdn_s):
    v = pl.program_id(0)
    lo = lo_ref[v]
    hi = hi_ref[v]

    @pl.when(efirst_ref[v] == 1)
    def _():
        for c in range(D_MODEL // FFN_FC):
            rs = slice(c * FFN_FC, (c + 1) * FFN_FC)
            wgu_s[rs, :] = wgu_ref[rs, :].astype(BF16)
            wdn_s[rs, :] = wdn_ref[rs, :].astype(BF16)

    @pl.when(hi > lo)
    def _():
        xw = x_ref[...]
        xl = _unpack_lo(xw).astype(BF16)
        xh = _unpack_hi(xw).astype(BF16)
        acc = jnp.zeros((x_ref.shape[0], D_MODEL), F32)
        for c in range(D_FF // FFN_FC):
            gs = slice(c * FFN_FC, (c + 1) * FFN_FC)
            us = slice(D_FF + c * FFN_FC, D_FF + (c + 1) * FFN_FC)
            gate = (jnp.dot(xl, wgu_s[:HALF, gs], preferred_element_type=F32)
                    + jnp.dot(xh, wgu_s[HALF:, gs], preferred_element_type=F32) + bgu_ref[:, gs])
            up = (jnp.dot(xl, wgu_s[:HALF, us], preferred_element_type=F32)
                  + jnp.dot(xh, wgu_s[HALF:, us], preferred_element_type=F32) + bgu_ref[:, us])
            gate = jnp.minimum(gate, SWIGLU_LIMIT)
            up = jnp.clip(up, -SWIGLU_LIMIT, SWIGLU_LIMIT)
            act = (up + 1.0) * (gate * jax.nn.sigmoid(gate * SWIGLU_ALPHA))
            acc = acc + jnp.dot(act.astype(BF16), wdn_s[gs, :], preferred_element_type=F32)
        new = _pack_halves(acc + bdn_ref[...])
        row = lax.broadcasted_iota(I32, new.shape, 0)
        keep = jnp.where(first_ref[v] == 1, jnp.zeros_like(new), y_ref[...])
        y_ref[...] = jnp.where(row >= lo, jnp.where(row < hi, new, keep), keep)


def _ffn(visits, xs, layer, wgu, bgu, wdn, bdn):
    n_rows, hw = xs.shape
    n_vis = visits[0].shape[0]
    blk = lambda v, vb, *_: (vb[v], 0)
    exp = lambda v, vb, ve, *_: (layer, ve[v], 0, 0)
    grid_spec = pltpu.PrefetchScalarGridSpec(
        num_scalar_prefetch=len(visits),
        grid=(n_vis,),
        in_specs=[pl.BlockSpec((FFN_TM, hw), blk),
                  pl.BlockSpec((None, None, D_MODEL, 2 * D_FF), exp),
                  pl.BlockSpec((None, None, 1, 2 * D_FF), exp),
                  pl.BlockSpec((None, None, D_FF, D_MODEL), exp),
                  pl.BlockSpec((None, None, 1, D_MODEL), exp)],
        out_specs=pl.BlockSpec((FFN_TM, hw), blk),
        scratch_shapes=[pltpu.VMEM((D_MODEL, 2 * D_FF), BF16), pltpu.VMEM((D_FF, D_MODEL), BF16)],
    )
    return pl.pallas_call(
        _ffn_kernel,
        grid_spec=grid_spec,
        out_shape=jax.ShapeDtypeStruct((n_rows, hw), I32),
        compiler_params=_cparams(("arbitrary",)),
        name="moe_ffn",
    )(*visits, xs, wgu, bgu, wdn, bdn)


def _combine_kernel(yg_ref, h_ref, gate_ref, g_ref, b_ref, o_ref):
    gate = gate_ref[...]
    h = h_ref[...]
    zl = DN_ALPHA * h[:, :HALF]
    zh = DN_ALPHA * h[:, HALF:]
    for k in range(TOP_K):
        w = yg_ref[k]
        gk = gate[:, k:k + 1]
        zl = zl + gk * _unpack_lo(w)
        zh = zh + gk * _unpack_hi(w)
    mu = (jnp.sum(zl, axis=-1, keepdims=True) + jnp.sum(zh, axis=-1, keepdims=True)) * (1.0 / D_MODEL)
    zl = zl - mu
    zh = zh - mu
    var = (jnp.sum(zl * zl, axis=-1, keepdims=True)
           + jnp.sum(zh * zh, axis=-1, keepdims=True)) * (1.0 / D_MODEL)
    r = lax.rsqrt(var + LN_EPS)
    o_ref[:, :HALF] = zl * r * g_ref[:, :HALF] + b_ref[:, :HALF]
    o_ref[:, HALF:] = zh * r * g_ref[:, HALF:] + b_ref[:, HALF:]


def _combine(yg, h1, gates, g, b):
    m, d = h1.shape
    tm = min(COMB_TM, m)
    return pl.pallas_call(
        _combine_kernel,
        grid=(m // tm,),
        in_specs=[pl.BlockSpec((TOP_K, tm, HALF), lambda i: (0, i, 0)),
                  pl.BlockSpec((tm, d), lambda i: (i, 0)),
                  pl.BlockSpec((tm, LANES), lambda i: (i, 0)),
                  _full(g, 1), _full(b, 1)],
        out_specs=pl.BlockSpec((tm, d), lambda i: (i, 0)),
        out_shape=jax.ShapeDtypeStruct((m, d), F32),
        compiler_params=_cparams(("parallel",)),
        name="moe_combine",
    )(yg, h1, gates, g, b)


def _routing_tables(top_e, rank, counts):
    n_tok = top_e.shape[0]
    n_blocks = n_tok * TOP_K // FFN_TM
    n_vis = n_blocks + N_EXPERTS - 1
    end = jnp.cumsum(counts)
    start = end - counts
    onehot = top_e[:, :, None] == jnp.arange(N_EXPERTS, dtype=I32)[None, None, :]
    pos = (rank + jnp.sum(jnp.where(onehot, start[None, None, :], 0), axis=-1)).astype(I32)

    first = start // FFN_TM
    last = jnp.maximum(end - 1, 0) // FFN_TM
    nvis = jnp.where(counts > 0, last - first + 1, 0)
    vend = jnp.cumsum(nvis)
    vstart = vend - nvis
    total = vend[-1]
    v = jnp.arange(n_vis, dtype=I32)
    valid = v < total
    ev = jnp.sum((vend[None, :] <= jnp.minimum(v, total - 1)[:, None]).astype(I32), axis=1)
    ev = jnp.minimum(ev, N_EXPERTS - 1)
    pick = lambda tbl: jnp.sum(jnp.where(ev[:, None] == jnp.arange(N_EXPERTS, dtype=I32)[None, :],
                                         tbl[None, :], 0), axis=1)
    bv = jnp.where(valid, pick(first) + v - pick(vstart), n_blocks - 1).astype(I32)
    lo = jnp.where(valid, jnp.maximum(pick(start), bv * FFN_TM) - bv * FFN_TM, 0).astype(I32)
    hi = jnp.where(valid, jnp.minimum(pick(end), (bv + 1) * FFN_TM) - bv * FFN_TM, 0).astype(I32)
    prev_b = jnp.concatenate([jnp.full((1,), -1, I32), bv[:-1]])
    prev_e = jnp.concatenate([jnp.full((1,), -1, I32), ev[:-1]])
    fi = jnp.where(valid & (bv != prev_b), 1, 0).astype(I32)
    efi = jnp.where(ev != prev_e, 1, 0).astype(I32)
    return pos, (bv, ev.astype(I32), lo, hi, fi, efi)


def _moe(routed, layer, wgu, bgu, wdn, bdn, g, b):
    h1, h1p, te, gates, rk, cnt = routed
    n_tok = h1.shape[0]
    pos, visits = _routing_tables(te[:, :TOP_K], rk[:, :TOP_K], cnt[0, :N_EXPERTS].astype(I32))
    xs = _sc_scatter(h1p, pos)
    y = _ffn(visits, xs, layer, wgu, bgu, wdn, bdn)
    yg = _sc_gather(y, jnp.transpose(pos).reshape(-1)).reshape(TOP_K, n_tok, HALF)
    return _combine(yg, h1, gates, g, b)


def _row(a):
    return a.reshape(1, -1).astype(F32)


def _even_layer(h2, bsz, seq, w_in, q_norm, kv_norm, w_uq, w_uk, w_uv, w_iq, idx_g, idx_b, v_g, v_b,
                w_s, b_s, w_out, ln_g, ln_b, rw, rb):
    o4 = D_CQ + D_C + D_IDX + H_IDX
    w_in_p = jnp.concatenate(
        [w_in[:, :o4], jnp.zeros((D_MODEL, E_IN_PAD - E_IN_EVEN), F32), w_in[:, o4:]], axis=1)
    pad_idx = lambda a: jnp.pad(_row(a), ((0, 0), (0, LANES - D_IDX)))
    cq, ckv, kidx, widx, uv = _proj_even(h2, w_in_p.astype(BF16), _row(q_norm), _row(kv_norm),
                                         pad_idx(idx_g), pad_idx(idx_b))
    as3 = lambda a: a.reshape(bsz, seq, a.shape[-1])

    n_sel = min(TOPK_MAX, seq // 4)
    tq = min(DSA_TQ, seq)
    wuq = jnp.transpose(w_uq, (1, 0, 2)).astype(BF16)
    wiq = jnp.pad(jnp.transpose(w_iq, (1, 0, 2)),
                  ((0, 0), (0, 0), (0, LANES - D_IDX))).astype(BF16)
    wuk = w_uk.astype(BF16)
    eye = jnp.eye(H_A, dtype=F32)
    wuv = (w_uv[:, :, None, :] * eye[:, None, :, None]).reshape(H_A, D_C, H_A * DH_A).astype(BF16)
    o_a = jnp.concatenate(
        [_dsa_tile(as3(cq), as3(ckv), as3(kidx), as3(widx), qt, tq, n_sel, wuq, wuk, wiq, wuv)
         for qt in range(seq // tq)], axis=1)

    gbias = jnp.repeat(jnp.transpose(b_s), CG_B, axis=1)
    o_b = _gmlp(as3(uv), _row(v_g), _row(v_b), w_s, gbias)

    n_a = H_A * DH_A
    w_out_b = w_out.astype(BF16)
    return _outln([o_a.reshape(bsz * seq, n_a), o_b.reshape(bsz * seq, GMLP_WIDTH)],
                  [w_out_b[:n_a], w_out_b[n_a:]], h2, _row(ln_g), _row(ln_b), rw, rb)


def _odd_layer(h2, bsz, seq, layer, w_in, lq1, lk1, lq2, lk2, subln_g, w_out, ln_g, ln_b, rw, rb):
    lam_init = 0.8 - 0.6 * math.exp(-0.3 * layer)
    qk, vt3 = _proj_odd(h2, w_in.astype(BF16), seq)
    qk3 = qk.reshape(bsz, seq, 2 * QK_W_C)
    tq = min(DIFF_TQ, seq)
    kpos = jnp.stack([_alibi_key_cols(seq, DH_C), _alibi_key_cols(seq, 0)])
    qa0 = _alibi_query_cols(H_C)
    qa = jnp.stack([jnp.roll(qa0, DH_C, axis=-1), qa0], axis=1)
    diag = jnp.transpose(_alibi_diag(H_C, tq), (0, 2, 1))
    g_col = subln_g.reshape(-1, 1).astype(F32)
    o = jnp.concatenate(
        [_diff_tile(qk3, vt3, qt, tq, lam_init, kpos, qa, diag, _row(lq1), _row(lk1), _row(lq2),
                    _row(lk2), g_col) for qt in range(seq // tq)], axis=1)
    return _outln([o.reshape(bsz * seq, QK_W_C)], [w_out.astype(BF16)], h2, _row(ln_g), _row(ln_b),
                  rw, rb)


def kernel(x, ev_w_in, ev_q_norm, ev_kv_norm, ev_w_uq, ev_w_uk, ev_w_uv, ev_w_iq, ev_idx_k_g, ev_idx_k_b, ev_v_norm_g, ev_v_norm_b, ev_w_s, ev_b_s, ev_w_out, od_w_in, od_lambda_q1, od_lambda_k1, od_lambda_q2, od_lambda_k2, od_subln_g, od_w_out, ln1_g, ln1_b, ln2_g, ln2_b, router_w, router_b, exp_w_gu, exp_b_gu, exp_w_dn, exp_b_dn):
    bsz, seq, d = x.shape
    h2 = x.reshape(bsz * seq, d)
    for l in range(DEPTH):
        j = l // 2
        rw = jnp.pad(router_w[l], ((0, 0), (0, LANES - N_EXPERTS)))
        rb = jnp.pad(router_b[l], (0, LANES - N_EXPERTS), constant_values=-jnp.inf).reshape(1, LANES)
        if l % 2 == 0:
            routed = _even_layer(h2, bsz, seq, ev_w_in[j], ev_q_norm[j], ev_kv_norm[j],
                                 ev_w_uq[j], ev_w_uk[j], ev_w_uv[j], ev_w_iq[j], ev_idx_k_g[j],
                                 ev_idx_k_b[j], ev_v_norm_g[j], ev_v_norm_b[j], ev_w_s[j],
                                 ev_b_s[j], ev_w_out[j], ln1_g[l], ln1_b[l], rw, rb)
        else:
            routed = _odd_layer(h2, bsz, seq, l, od_w_in[j], od_lambda_q1[j], od_lambda_k1[j],
                                od_lambda_q2[j], od_lambda_k2[j], od_subln_g[j], od_w_out[j],
                                ln1_g[l], ln1_b[l], rw, rb)
        h2 = _moe(routed, l, exp_w_gu, exp_b_gu[:, :, None, :], exp_w_dn, exp_b_dn[:, :, None, :],
                  _row(ln2_g[l]), _row(ln2_b[l]))
    return h2.reshape(bsz, seq, d)
```

```python
import functools
import math

import jax
import jax.numpy as jnp
from jax import lax
from jax.experimental import pallas as pl
from jax.experimental.pallas import tpu as pltpu
from jax.experimental.pallas import tpu_sc as plsc

F32 = jnp.float32
BF16 = jnp.bfloat16
I32 = jnp.int32

D_MODEL = 1024
DEPTH = 4
CHUNK = 64
CHUNK_SHIFT = 6
H_A = 8
DH_A = 64
D_CQ = 256
D_C = 128
H_IDX = 4
D_IDX = 64
TOPK_MAX = 256
GMLP_CHUNK = 128
G_B = 8
GMLP_WIDTH = 512
CG_B = GMLP_WIDTH // G_B
H_C = 8
DH_C = 64
QK_W_C = H_C * 2 * DH_C
N_EXPERTS = 32
TOP_K = 4
D_FF = 1024
SWIGLU_ALPHA = 1.702
SWIGLU_LIMIT = 7.0
DN_ALPHA = (2 * DEPTH) ** 0.25
E_IN_EVEN = D_CQ + D_C + D_IDX + H_IDX + 2 * GMLP_WIDTH
LN_EPS = 1e-5
NEG = -1e30
LOG2E = 1.4426950408889634

LANES = 128
E_IN_PAD = 1536
INT_MIN = -(2 ** 31)
HALF = D_MODEL // 2
HI_MASK = -65536
SC_WIN = 64
POS_RADIX = 256
N_POS_COLS = 6

PROJ_TM = 512
DSA_TQ = 256
DIFF_TQ = 512
GMLP_TG = 512
FFN_TM = 512
FFN_FC = 512
COMB_TM = 512
BATCH_GROUPS = 2
VMEM_LIMIT = 56 * 1024 * 1024


def _cparams(sem):
    return pltpu.CompilerParams(dimension_semantics=sem, vmem_limit_bytes=VMEM_LIMIT)


def _full(a, n_grid):
    zeros = (0,) * a.ndim
    return pl.BlockSpec(a.shape, lambda *_: zeros)


def _alibi_slopes_l2(n):
    return jnp.exp2(-8.0 * jnp.arange(1, n + 1, dtype=F32) / n) * LOG2E


def _alibi_query_cols(n):
    s = _alibi_slopes_l2(n)
    hi = s.astype(BF16).astype(F32)
    mid = (s - hi).astype(BF16).astype(F32)
    lo = (s - hi - mid).astype(BF16).astype(F32)
    cols = jnp.stack([hi * POS_RADIX, hi, mid * POS_RADIX, mid, lo * POS_RADIX, lo], axis=-1)
    return jnp.pad(cols, ((0, 0), (0, LANES - N_POS_COLS))).reshape(n, 1, LANES)


def _alibi_key_cols(seq, lane0):
    s = jnp.arange(seq, dtype=I32)
    a = (s // POS_RADIX).astype(F32)
    b = (s % POS_RADIX).astype(F32)
    cols = jnp.stack([a, b, a, b, a, b], axis=-1)
    return jnp.pad(cols, ((0, 0), (lane0, LANES - N_POS_COLS - lane0))).astype(BF16)


def _alibi_diag(n, tq):
    t = jnp.arange(tq, dtype=I32)[:, None]
    s = jnp.arange(tq, dtype=I32)[None, :]
    ok = (s >> CHUNK_SHIFT) <= (t >> CHUNK_SHIFT)
    ahead = jnp.maximum(s - t, 0).astype(F32)
    corr = -2.0 * _alibi_slopes_l2(n)[:, None, None] * ahead[None]
    return jnp.where(ok[None], corr, NEG)


def _proj_even_kernel(x_ref, w_ref, qg_ref, kvg_ref, ig_ref, ib_ref,
                      cq_ref, ckv_ref, kidx_ref, widx_ref, uv_ref):
    acc = jnp.dot(x_ref[...].astype(BF16), w_ref[...], preferred_element_type=F32)
    cq = acc[:, :D_CQ]
    cq_ref[...] = (cq * lax.rsqrt(jnp.mean(cq * cq, axis=-1, keepdims=True) + LN_EPS)
                   * qg_ref[...]).astype(cq_ref.dtype)
    kv = acc[:, D_CQ:D_CQ + D_C]
    ckv_ref[...] = (kv * lax.rsqrt(jnp.mean(kv * kv, axis=-1, keepdims=True) + LN_EPS)
                    * kvg_ref[...]).astype(ckv_ref.dtype)
    blk = acc[:, D_CQ + D_C:D_CQ + D_C + LANES]
    lane = lax.broadcasted_iota(I32, blk.shape, 1)
    is_k = lane < D_IDX
    mu = jnp.sum(jnp.where(is_k, blk, 0.0), axis=-1, keepdims=True) * (1.0 / D_IDX)
    kc = jnp.where(is_k, blk - mu, 0.0)
    var = jnp.sum(kc * kc, axis=-1, keepdims=True) * (1.0 / D_IDX)
    kidx_ref[...] = jnp.where(is_k, kc * lax.rsqrt(var + LN_EPS) * ig_ref[...] + ib_ref[...],
                              0.0).astype(kidx_ref.dtype)
    widx_ref[...] = jnp.where(is_k, 0.0, blk * (H_IDX ** -0.5))
    uv_ref[...] = acc[:, E_IN_PAD - 2 * GMLP_WIDTH:]


def _proj_even(x2, w, qg, kvg, ig, ib):
    m, k = x2.shape
    tm = min(PROJ_TM, m)
    row = lambda n: pl.BlockSpec((tm, n), lambda i: (i, 0))
    consts = (qg, kvg, ig, ib)
    return pl.pallas_call(
        _proj_even_kernel,
        grid=(m // tm,),
        in_specs=[row(k), _full(w, 1)] + [_full(c, 1) for c in consts],
        out_specs=[row(D_CQ), row(D_C), row(LANES), row(LANES), row(2 * GMLP_WIDTH)],
        out_shape=[jax.ShapeDtypeStruct((m, D_CQ), BF16),
                   jax.ShapeDtypeStruct((m, D_C), BF16),
                   jax.ShapeDtypeStruct((m, LANES), BF16),
                   jax.ShapeDtypeStruct((m, LANES), F32),
                   jax.ShapeDtypeStruct((m, 2 * GMLP_WIDTH), F32)],
        compiler_params=_cparams(("parallel",)),
        name="proj_even",
    )(x2, w, *consts)


def _proj_odd_kernel(x_ref, w_ref, qk_ref, vt_ref):
    acc = jnp.dot(x_ref[...].astype(BF16), w_ref[...], preferred_element_type=F32)
    qk_ref[:, :QK_W_C] = (acc[:, :QK_W_C] * (DH_C ** -0.5 * LOG2E)).astype(qk_ref.dtype)
    qk_ref[:, QK_W_C:] = acc[:, QK_W_C:2 * QK_W_C].astype(qk_ref.dtype)
    w = 2 * DH_C
    for h in range(H_C):
        v = acc[:, 2 * QK_W_C + h * w:2 * QK_W_C + (h + 1) * w]
        vt_ref[h * w:(h + 1) * w, :] = v.T.astype(vt_ref.dtype)


def _proj_odd(x2, w, seq):
    m, k = x2.shape
    tm = min(PROJ_TM, seq)
    per_seq = seq // tm
    return pl.pallas_call(
        _proj_odd_kernel,
        grid=(m // tm,),
        in_specs=[pl.BlockSpec((tm, k), lambda i: (i, 0)), _full(w, 1)],
        out_specs=[pl.BlockSpec((tm, 2 * QK_W_C), lambda i: (i, 0)),
                   pl.BlockSpec((None, QK_W_C, tm), lambda i: (i // per_seq, 0, i % per_seq))],
        out_shape=[jax.ShapeDtypeStruct((m, 2 * QK_W_C), BF16),
                   jax.ShapeDtypeStruct((m // seq, QK_W_C, seq), BF16)],
        compiler_params=_cparams(("parallel",)),
        name="proj_odd",
    )(x2, w)


_NT = (((1,), (1,)), ((), ()))


COL_GROUPS = 8


def _col_reduce(x, red, comb):
    step = x.shape[0] // COL_GROUPS
    vals = [red(x[i * step:(i + 1) * step], axis=0, keepdims=True) for i in range(COL_GROUPS)]
    while len(vals) > 1:
        vals = [comb(vals[i], vals[i + 1]) for i in range(0, len(vals), 2)]
    return vals[0]


def _softmax_pv_t(parts, vt_parts):
    m = None
    for lg in parts:
        mx = _col_reduce(lg, jnp.max, jnp.maximum)
        m = mx if m is None else jnp.maximum(m, mx)
    l = None
    o = None
    for lg, vt in zip(parts, vt_parts):
        p = jnp.exp2(lg - m)
        s = _col_reduce(p, jnp.sum, jnp.add)
        d = jnp.dot(vt, p.astype(BF16), preferred_element_type=F32)
        l = s if l is None else l + s
        o = d if o is None else o + d
    return o * (1.0 / l)


def _softmax_pv(parts, kv_parts):
    m = None
    for lg in parts:
        mx = jnp.max(lg, axis=-1, keepdims=True)
        m = mx if m is None else jnp.maximum(m, mx)
    l = None
    o = None
    for lg, kv in zip(parts, kv_parts):
        p = jnp.exp2(lg - m)
        s = jnp.sum(p, axis=-1, keepdims=True)
        d = jnp.dot(p.astype(BF16), kv, preferred_element_type=F32)
        l = s if l is None else l + s
        o = d if o is None else o + d
    return o * (1.0 / l)


def _dsa_kernel(cq_ref, ckv_ref, kidx_ref, widx_ref, wuq_ref, wuk_ref, wiq_ref, wuv_ref, srow_ref,
                diag_ref, o_ref, bias_ref, *, q0, n_sel):
    tq = cq_ref.shape[0]
    sk = ckv_ref.shape[0]
    cqb = cq_ref[...]
    kidx = kidx_ref[...]
    widx = widx_ref[...]

    score = None
    for h in range(H_IDX):
        qi = jnp.dot(cqb, wiq_ref[h], preferred_element_type=F32).astype(BF16)
        r = lax.dot_general(qi, kidx, _NT, preferred_element_type=F32)
        term = jnp.maximum(r * (D_IDX ** -0.5), 0.0) * widx[:, D_IDX + h:D_IDX + h + 1]
        score = term if score is None else score + term
    t_pos = q0 + lax.broadcasted_iota(I32, (tq, sk), 0)
    s_pos = lax.broadcasted_iota(I32, (tq, sk), 1)
    allowed = (s_pos >> CHUNK_SHIFT) <= (t_pos >> CHUNK_SHIFT)
    score = jnp.where(allowed, score, NEG)

    bits = lax.bitcast_convert_type(score, I32)
    key = jnp.where(bits < 0, bits ^ 0x7FFFFFFF, bits)
    key = jnp.where(key == -1, 0, key)

    def count(mask):
        return jnp.sum(jnp.where(mask, 1.0, 0.0), axis=-1, keepdims=True)

    nsel = float(n_sel)
    cur = jnp.where(count(key >= 0) >= nsel, 0, INT_MIN).astype(I32)

    def search(i, cur):
        cand = cur | lax.shift_left(jnp.int32(1), 30 - i)
        return jnp.where(count(key >= cand) >= nsel, cand, cur)

    cur = lax.fori_loop(0, 31, search, cur)

    need = nsel - count(key > cur)
    ri = lax.broadcasted_iota(I32, (LANES, LANES), 0)
    ci = lax.broadcasted_iota(I32, (LANES, LANES), 1)
    tri = jnp.where(ri < ci, 1.0, 0.0).astype(BF16)
    ones_b = jnp.ones((LANES, LANES), BF16)
    off = jnp.zeros((tq, 1), F32)
    for c in range(sk // LANES):
        sl = slice(c * LANES, (c + 1) * LANES)
        keyc = key[:, sl]
        eqc = jnp.where(keyc == cur, 1.0, 0.0)
        eqb = eqc.astype(BF16)
        rank = jnp.dot(eqb, tri, preferred_element_type=F32) + off
        sel = jnp.where(keyc > cur, 1.0, jnp.where(rank < need, eqc, 0.0))
        bias_ref[:, sl] = jnp.where(sel > 0.5, 0.0, NEG)
        off = off + jnp.dot(eqb, ones_b, preferred_element_type=F32)[:, :1]

    n_lo = sk - tq
    kv_parts = ([ckv_ref[:n_lo, :]] if n_lo else []) + [ckv_ref[n_lo:, :]]
    acc = jnp.zeros((tq, H_A * DH_A), F32)
    for h in range(H_A):
        qh = jnp.dot(cqb, wuq_ref[h], preferred_element_type=F32).astype(BF16)
        ql = (jnp.dot(qh, wuk_ref[h], preferred_element_type=F32)
              * (DH_A ** -0.5 * LOG2E)).astype(BF16)
        parts = []
        if n_lo:
            lg = lax.dot_general(ql, kv_parts[0], _NT, preferred_element_type=F32)
            parts.append(lg + bias_ref[:, :n_lo] + srow_ref[h][:, :n_lo])
        lg = lax.dot_general(ql, kv_parts[-1], _NT, preferred_element_type=F32)
        parts.append(lg + bias_ref[:, n_lo:] + diag_ref[h])
        ol = _softmax_pv(parts, kv_parts)
        acc = acc + jnp.dot(ol.astype(BF16), wuv_ref[h], preferred_element_type=F32)
    o_ref[...] = acc.astype(o_ref.dtype)


def _dsa_tile(cq3, ckv3, kidx3, widx3, qt, tq, n_sel, wuq, wuk, wiq, wuv):
    b, s, _ = cq3.shape
    sk = (qt + 1) * tq
    slopes = _alibi_slopes_l2(H_A)
    srow = (slopes[:, None] * jnp.arange(sk, dtype=F32)[None, :]).reshape(H_A, 1, sk)
    diag = _alibi_diag(H_A, tq) + srow[:, :, sk - tq:]
    consts = (wuq, wuk, wiq, wuv, srow, diag)
    return pl.pallas_call(
        functools.partial(_dsa_kernel, q0=qt * tq, n_sel=n_sel),
        grid=(b,),
        in_specs=[pl.BlockSpec((None, tq, D_CQ), lambda i: (i, qt, 0)),
                  pl.BlockSpec((None, sk, D_C), lambda i: (i, 0, 0)),
                  pl.BlockSpec((None, sk, LANES), lambda i: (i, 0, 0)),
                  pl.BlockSpec((None, tq, LANES), lambda i: (i, qt, 0))]
        + [_full(c, 1) for c in consts],
        out_specs=pl.BlockSpec((None, tq, H_A * DH_A), lambda i: (i, 0, 0)),
        out_shape=jax.ShapeDtypeStruct((b, tq, H_A * DH_A), BF16),
        scratch_shapes=[pltpu.VMEM((tq, sk), F32)],
        compiler_params=_cparams(("parallel",)),
        name=f"dsa_q{qt}",
    )(cq3, ckv3, kidx3, widx3, *consts)


def _gelu(x):
    return 0.5 * x * (1.0 + jnp.tanh(0.7978845608028654 * (x + 0.044715 * (x * x * x))))


def _gmlp_kernel(u_ref, v_ref, vg_ref, vb_ref, ws_ref, bias_ref, o_ref):
    tg = u_ref.shape[0]
    ri = lax.broadcasted_iota(I32, (GMLP_CHUNK, GMLP_CHUNK), 0)
    ci = lax.broadcasted_iota(I32, (GMLP_CHUNK, GMLP_CHUNK), 1)
    tril = ri >= ci
    grp = lax.broadcasted_iota(I32, (GMLP_CHUNK, GMLP_WIDTH), 1) // CG_B
    ws = [jnp.where(tril, ws_ref[g], 0.0).astype(BF16) for g in range(G_B)]
    for c in range(tg // GMLP_CHUNK):
        sl = slice(c * GMLP_CHUNK, (c + 1) * GMLP_CHUNK)
        u = _gelu(u_ref[sl, :])
        v = _gelu(v_ref[sl, :])
        mu = jnp.mean(v, axis=-1, keepdims=True)
        vc = v - mu
        var = jnp.mean(vc * vc, axis=-1, keepdims=True)
        vn = (vc * lax.rsqrt(var + LN_EPS) * vg_ref[...] + vb_ref[...]).astype(BF16)
        mixed = bias_ref[...]
        for g in range(G_B):
            r = jnp.dot(ws[g], vn, preferred_element_type=F32)
            mixed = mixed + jnp.where(grp == g, r, 0.0)
        o_ref[sl, :] = (u * mixed).astype(o_ref.dtype)


def _gmlp(uv3, vg, vb, ws, bias):
    b, s, _ = uv3.shape
    tg = min(GMLP_TG, s)
    consts = (vg, vb, ws, bias)
    return pl.pallas_call(
        _gmlp_kernel,
        grid=(b, s // tg),
        in_specs=[pl.BlockSpec((None, tg, GMLP_WIDTH), lambda i, j: (i, j, 0)),
                  pl.BlockSpec((None, tg, GMLP_WIDTH), lambda i, j: (i, j, 1))]
        + [_full(c, 2) for c in consts],
        out_specs=pl.BlockSpec((None, tg, GMLP_WIDTH), lambda i, j: (i, j, 0)),
        out_shape=jax.ShapeDtypeStruct((b, s, GMLP_WIDTH), BF16),
        compiler_params=_cparams(("parallel", "parallel")),
        name="gmlp",
    )(uv3, uv3, *consts)


def _diff_kernel(q_ref, k_ref, vt_ref, kpos_ref, qa_ref, diag_ref, lq1_ref, lk1_ref, lq2_ref, lk2_ref,
                 g_ref, o_ref, *, lam_init):
    tq = q_ref.shape[0]
    sk = k_ref.shape[0]
    n_lo = sk - tq
    lam = (jnp.exp(jnp.sum(lq1_ref[...] * lk1_ref[...], axis=-1, keepdims=True))
           - jnp.exp(jnp.sum(lq2_ref[...] * lk2_ref[...], axis=-1, keepdims=True)) + lam_init)
    q = q_ref[...]
    k = k_ref[...]
    vt_parts = ([vt_ref[:, :n_lo]] if n_lo else []) + [vt_ref[:, n_lo:]]
    q_lane = lax.broadcasted_iota(I32, q.shape, 1)
    k_lane = lax.broadcasted_iota(I32, k.shape, 1)
    outs = []
    for m in range(2):
        q_own = (q_lane < DH_C) if m == 0 else (q_lane >= DH_C)
        k_own = (k_lane < DH_C) if m == 0 else (k_lane >= DH_C)
        qm = jnp.where(q_own, q, jnp.broadcast_to(qa_ref[m], q.shape).astype(BF16))
        km = jnp.where(k_own, k, kpos_ref[m])
        parts = []
        if n_lo:
            parts.append(lax.dot_general(km[:n_lo], qm, _NT, preferred_element_type=F32))
        lg = lax.dot_general(km[n_lo:], qm, _NT, preferred_element_type=F32)
        parts.append(lg + diag_ref[...])
        outs.append(_softmax_pv_t(parts, vt_parts))
    o = outs[0] - lam * outs[1]
    o = o * lax.rsqrt(jnp.mean(o * o, axis=0, keepdims=True) + LN_EPS) * g_ref[...]
    o_ref[...] = (o * (1.0 - lam_init)).T.astype(o_ref.dtype)


def _diff_tile(qk3, vt3, qt, tq, lam_init, kpos, qa, diag, lq1, lk1, lq2, lk2, g):
    b, s, _ = qk3.shape
    sk = (qt + 1) * tq
    w = 2 * DH_C
    consts = (lq1, lk1, lq2, lk2, g)
    return pl.pallas_call(
        functools.partial(_diff_kernel, lam_init=lam_init),
        grid=(b, H_C),
        in_specs=[pl.BlockSpec((None, tq, w), lambda i, j: (i, qt, j)),
                  pl.BlockSpec((None, sk, w), lambda i, j: (i, 0, H_C + j)),
                  pl.BlockSpec((None, w, sk), lambda i, j: (i, j, 0)),
                  pl.BlockSpec((2, sk, w), lambda i, j: (0, 0, 0)),
                  pl.BlockSpec((None, 2, 1, w), lambda i, j: (j, 0, 0, 0)),
                  pl.BlockSpec((None, tq, tq), lambda i, j: (j, 0, 0))]
        + [_full(c, 2) for c in consts],
        out_specs=pl.BlockSpec((None, tq, w), lambda i, j: (i, 0, j)),
        out_shape=jax.ShapeDtypeStruct((b, tq, H_C * w), BF16),
        compiler_params=_cparams(("parallel", "parallel")),
        name=f"diff_q{qt}",
    )(qk3, qk3, vt3, kpos, qa, diag, *consts)


def _pack_halves(x):
    lo = lax.bitcast_convert_type(x[:, :HALF].astype(BF16).astype(F32), I32)
    hi = lax.bitcast_convert_type(x[:, HALF:].astype(BF16).astype(F32), I32)
    return lax.shift_right_logical(lo, 16) | (hi & HI_MASK)


def _unpack_lo(w):
    return lax.bitcast_convert_type(lax.shift_left(w, 16), F32)


def _unpack_hi(w):
    return lax.bitcast_convert_type(w & HI_MASK, F32)


def _layer_norm_rows(z, g, b):
    mu = jnp.mean(z, axis=-1, keepdims=True)
    zc = z - mu
    var = jnp.mean(zc * zc, axis=-1, keepdims=True)
    return zc * lax.rsqrt(var + LN_EPS) * g + b


def _split2(x):
    hi = x.astype(BF16)
    mid = (x - hi.astype(F32)).astype(BF16)
    return hi, mid


def _outln_kernel(*refs, n_in):
    xs = refs[:n_in]
    ws = refs[n_in:2 * n_in]
    (h_ref, g_ref, b_ref, rw_ref, rb_ref, tri_ref,
     h1_ref, hp_ref, te_ref, tg_ref, rk_ref, cnt_ref) = refs[2 * n_in:]
    acc = None
    for x_ref, w_ref in zip(xs, ws):
        d = jnp.dot(x_ref[...], w_ref[...], preferred_element_type=F32)
        acc = d if acc is None else acc + d
    h1 = _layer_norm_rows(DN_ALPHA * h_ref[...] + acc, g_ref[...], b_ref[...])
    h1_ref[...] = h1
    hp_ref[...] = _pack_halves(h1)

    h_hi, h_mid = _split2(h1)
    dotf = lambda a, b: jnp.dot(a, b, preferred_element_type=F32)
    logits = dotf(h_hi, rw_ref[0]) + dotf(h_mid, rw_ref[0]) + dotf(h_hi, rw_ref[1]) + rb_ref[...]
    lane = lax.broadcasted_iota(I32, logits.shape, 1)
    vals, idxs = [], []
    for _ in range(TOP_K):
        mx = jnp.max(logits, axis=-1, keepdims=True)
        ix = jnp.min(jnp.where(logits == mx, lane, LANES), axis=-1, keepdims=True)
        vals.append(mx)
        idxs.append(ix)
        logits = jnp.where(lane == ix, -jnp.inf, logits)
    es = [jnp.exp(v - vals[0]) for v in vals]
    den = es[0] + es[1] + es[2] + es[3]

    @pl.when(pl.program_id(0) == 0)
    def _():
        cnt_ref[...] = jnp.zeros_like(cnt_ref)

    member = jnp.zeros(lane.shape, F32)
    for k in range(TOP_K):
        member = jnp.where(lane == idxs[k], 1.0, member)
    before = jnp.dot(tri_ref[...], member.astype(BF16), preferred_element_type=F32) + cnt_ref[...]
    te = jnp.zeros(lane.shape, I32)
    tg = jnp.zeros(lane.shape, F32)
    rk = jnp.zeros(lane.shape, F32)
    for k in range(TOP_K):
        rank_k = jnp.sum(jnp.where(lane == idxs[k], before, 0.0), axis=-1, keepdims=True)
        te = jnp.where(lane == k, idxs[k], te)
        tg = jnp.where(lane == k, es[k] / den, tg)
        rk = jnp.where(lane == k, rank_k, rk)
    te_ref[...] = te
    tg_ref[...] = tg
    rk_ref[...] = rk.astype(I32)
    cnt_ref[...] += jnp.sum(member, axis=0, keepdims=True)


def _outln(xs, ws, h2, g, b, rw, rb):
    m = h2.shape[0]
    tm = min(PROJ_TM, m)
    n_in = len(xs)
    row = lambda a: pl.BlockSpec((tm, a.shape[1]), lambda i: (i, 0))
    ri = lax.broadcasted_iota(I32, (tm, tm), 0)
    ci = lax.broadcasted_iota(I32, (tm, tm), 1)
    tri = (ci < ri).astype(BF16)
    rw3 = jnp.stack(_split2(rw))
    return pl.pallas_call(
        functools.partial(_outln_kernel, n_in=n_in),
        grid=(m // tm,),
        in_specs=[row(x) for x in xs] + [_full(w, 1) for w in ws]
        + [row(h2), _full(g, 1), _full(b, 1), _full(rw3, 1), _full(rb, 1), _full(tri, 1)],
        out_specs=[pl.BlockSpec((tm, D_MODEL), lambda i: (i, 0)),
                   pl.BlockSpec((tm, HALF), lambda i: (i, 0)),
                   pl.BlockSpec((tm, LANES), lambda i: (i, 0)),
                   pl.BlockSpec((tm, LANES), lambda i: (i, 0)),
                   pl.BlockSpec((tm, LANES), lambda i: (i, 0)),
                   pl.BlockSpec((1, LANES), lambda i: (0, 0))],
        out_shape=[jax.ShapeDtypeStruct((m, D_MODEL), F32),
                   jax.ShapeDtypeStruct((m, HALF), I32),
                   jax.ShapeDtypeStruct((m, LANES), I32),
                   jax.ShapeDtypeStruct((m, LANES), F32),
                   jax.ShapeDtypeStruct((m, LANES), I32),
                   jax.ShapeDtypeStruct((1, LANES), F32)],
        compiler_params=_cparams(("arbitrary",)),
        name="outln_router",
    )(*xs, *ws, h2, g, b, rw3, rb, tri)


def _sc_mesh():
    return plsc.VectorSubcoreMesh(core_axis_name="core", subcore_axis_name="subcore")


def _pad_windows(idx2):
    return jnp.pad(idx2, ((0, 0), (0, LANES - SC_WIN)))


def _sc_scatter(x, pos):
    t, d = x.shape
    nw = t // SC_WIN
    idx = _pad_windows(pos.reshape(nw, SC_WIN, TOP_K).transpose(0, 2, 1).reshape(nw * TOP_K, SC_WIN))

    @pl.kernel(out_type=jax.ShapeDtypeStruct((t * TOP_K, d), x.dtype), mesh=_sc_mesh())
    def scatter_rows(x_hbm, i_hbm, o_hbm):
        def body(x_vmem, i_vmem):
            pltpu.sync_copy(x_vmem, o_hbm.at[i_vmem.at[0, pl.ds(0, SC_WIN)]])

        pltpu.emit_pipeline(
            body,
            grid=(nw * TOP_K,),
            in_specs=[pl.BlockSpec((SC_WIN, d), lambda i: (i // TOP_K, 0)),
                      pl.BlockSpec((1, LANES), lambda i: (i, 0))],
            out_specs=[],
            core_axis_name=("core", "subcore"),
            dimension_semantics=(pltpu.PARALLEL,),
        )(x_hbm, i_hbm)

    return scatter_rows(x, idx)


def _sc_gather(y, idx):
    n = idx.shape[0]
    d = y.shape[1]
    idx2 = _pad_windows(idx.reshape(n // SC_WIN, SC_WIN))

    @pl.kernel(out_type=jax.ShapeDtypeStruct((n, d), y.dtype), mesh=_sc_mesh())
    def gather_rows(y_hbm, i_hbm, o_hbm):
        def body(i_vmem, o_vmem):
            pltpu.sync_copy(y_hbm.at[i_vmem.at[0, pl.ds(0, SC_WIN)]], o_vmem)

        pltpu.emit_pipeline(
            body,
            grid=(n // SC_WIN,),
            in_specs=[pl.BlockSpec((1, LANES), lambda i: (i, 0))],
            out_specs=[pl.BlockSpec((SC_WIN, d), lambda i: (i, 0))],
            core_axis_name=("core", "subcore"),
            dimension_semantics=(pltpu.PARALLEL,),
        )(i_hbm, o_hbm)

    return gather_rows(y, idx2)


def _ffn_kernel(vb_ref, ve_ref, lo_ref, hi_ref, first_ref, efirst_ref, x_ref, wgu_ref, bgu_ref,
                wdn_ref, bdn_ref, y_ref, wgu_s, wdn_s):
    v = pl.program_id(0)
    lo = lo_ref[v]
    hi = hi_ref[v]

    @pl.when(efirst_ref[v] == 1)
    def _():
        for c in range(D_MODEL // FFN_FC):
            rs = slice(c * FFN_FC, (c + 1) * FFN_FC)
            wgu_s[rs, :] = wgu_ref[rs, :].astype(BF16)
            wdn_s[rs, :] = wdn_ref[rs, :].astype(BF16)

    @pl.when(hi > lo)
    def _():
        xw = x_ref[...]
        xl = _unpack_lo(xw).astype(BF16)
        xh = _unpack_hi(xw).astype(BF16)
        acc = jnp.zeros((x_ref.shape[0], D_MODEL), F32)
        for c in range(D_FF // FFN_FC):
            gs = slice(c * FFN_FC, (c + 1) * FFN_FC)
            us = slice(D_FF + c * FFN_FC, D_FF + (c + 1) * FFN_FC)
            gate = (jnp.dot(xl, wgu_s[:HALF, gs], preferred_element_type=F32)
                    + jnp.dot(xh, wgu_s[HALF:, gs], preferred_element_type=F32) + bgu_ref[:, gs])
            up = (jnp.dot(xl, wgu_s[:HALF, us], preferred_element_type=F32)
                  + jnp.dot(xh, wgu_s[HALF:, us], preferred_element_type=F32) + bgu_ref[:, us])
            gate = jnp.minimum(gate, SWIGLU_LIMIT)
            up = jnp.clip(up, -SWIGLU_LIMIT, SWIGLU_LIMIT)
            act = (up + 1.0) * (gate * jax.nn.sigmoid(gate * SWIGLU_ALPHA))
            acc = acc + jnp.dot(act.astype(BF16), wdn_s[gs, :], preferred_element_type=F32)
        new = _pack_halves(acc + bdn_ref[...])
        row = lax.broadcasted_iota(I32, new.shape, 0)
        keep = jnp.where(first_ref[v] == 1, jnp.zeros_like(new), y_ref[...])
        y_ref[...] = jnp.where(row >= lo, jnp.where(row < hi, new, keep), keep)


def _ffn(visits, xs, layer, wgu, bgu, wdn, bdn):
    n_rows, hw = xs.shape
    n_vis = visits[0].shape[0]
    blk = lambda v, vb, *_: (vb[v], 0)
    exp = lambda v, vb, ve, *_: (layer, ve[v], 0, 0)
    grid_spec = pltpu.PrefetchScalarGridSpec(
        num_scalar_prefetch=len(visits),
        grid=(n_vis,),
        in_specs=[pl.BlockSpec((FFN_TM, hw), blk),
                  pl.BlockSpec((None, None, D_MODEL, 2 * D_FF), exp),
                  pl.BlockSpec((None, None, 1, 2 * D_FF), exp),
                  pl.BlockSpec((None, None, D_FF, D_MODEL), exp),
                  pl.BlockSpec((None, None, 1, D_MODEL), exp)],
        out_specs=pl.BlockSpec((FFN_TM, hw), blk),
        scratch_shapes=[pltpu.VMEM((D_MODEL, 2 * D_FF), BF16), pltpu.VMEM((D_FF, D_MODEL), BF16)],
    )
    return pl.pallas_call(
        _ffn_kernel,
        grid_spec=grid_spec,
        out_shape=jax.ShapeDtypeStruct((n_rows, hw), I32),
        compiler_params=_cparams(("arbitrary",)),
        name="moe_ffn",
    )(*visits, xs, wgu, bgu, wdn, bdn)


def _combine_kernel(yg_ref, h_ref, gate_ref, g_ref, b_ref, o_ref):
    gate = gate_ref[...]
    h = h_ref[...]
    zl = DN_ALPHA * h[:, :HALF]
    zh = DN_ALPHA * h[:, HALF:]
    for k in range(TOP_K):
        w = yg_ref[k]
        gk = gate[:, k:k + 1]
        zl = zl + gk * _unpack_lo(w)
        zh = zh + gk * _unpack_hi(w)
    mu = (jnp.sum(zl, axis=-1, keepdims=True) + jnp.sum(zh, axis=-1, keepdims=True)) * (1.0 / D_MODEL)
    zl = zl - mu
    zh = zh - mu
    var = (jnp.sum(zl * zl, axis=-1, keepdims=True)
           + jnp.sum(zh * zh, axis=-1, keepdims=True)) * (1.0 / D_MODEL)
    r = lax.rsqrt(var + LN_EPS)
    o_ref[:, :HALF] = zl * r * g_ref[:, :HALF] + b_ref[:, :HALF]
    o_ref[:, HALF:] = zh * r * g_ref[:, HALF:] + b_ref[:, HALF:]


def _combine(yg, h1, gates, g, b):
    m, d = h1.shape
    tm = min(COMB_TM, m)
    return pl.pallas_call(
        _combine_kernel,
        grid=(m // tm,),
        in_specs=[pl.BlockSpec((TOP_K, tm, HALF), lambda i: (0, i, 0)),
                  pl.BlockSpec((tm, d), lambda i: (i, 0)),
                  pl.BlockSpec((tm, LANES), lambda i: (i, 0)),
                  _full(g, 1), _full(b, 1)],
        out_specs=pl.BlockSpec((tm, d), lambda i: (i, 0)),
        out_shape=jax.ShapeDtypeStruct((m, d), F32),
        compiler_params=_cparams(("parallel",)),
        name="moe_combine",
    )(yg, h1, gates, g, b)


def _routing_tables(top_e, rank, counts):
    n_tok = top_e.shape[0]
    n_blocks = n_tok * TOP_K // FFN_TM
    n_vis = n_blocks + N_EXPERTS - 1
    end = jnp.cumsum(counts)
    start = end - counts
    onehot = top_e[:, :, None] == jnp.arange(N_EXPERTS, dtype=I32)[None, None, :]
    pos = (rank + jnp.sum(jnp.where(onehot, start[None, None, :], 0), axis=-1)).astype(I32)

    first = start // FFN_TM
    last = jnp.maximum(end - 1, 0) // FFN_TM
    nvis = jnp.where(counts > 0, last - first + 1, 0)
    vend = jnp.cumsum(nvis)
    vstart = vend - nvis
    total = vend[-1]
    v = jnp.arange(n_vis, dtype=I32)
    valid = v < total
    ev = jnp.sum((vend[None, :] <= jnp.minimum(v, total - 1)[:, None]).astype(I32), axis=1)
    ev = jnp.minimum(ev, N_EXPERTS - 1)
    pick = lambda tbl: jnp.sum(jnp.where(ev[:, None] == jnp.arange(N_EXPERTS, dtype=I32)[None, :],
                                         tbl[None, :], 0), axis=1)
    bv = jnp.where(valid, pick(first) + v - pick(vstart), n_blocks - 1).astype(I32)
    lo = jnp.where(valid, jnp.maximum(pick(start), bv * FFN_TM) - bv * FFN_TM, 0).astype(I32)
    hi = jnp.where(valid, jnp.minimum(pick(end), (bv + 1) * FFN_TM) - bv * FFN_TM, 0).astype(I32)
    prev_b = jnp.concatenate([jnp.full((1,), -1, I32), bv[:-1]])
    prev_e = jnp.concatenate([jnp.full((1,), -1, I32), ev[:-1]])
    fi = jnp.where(valid & (bv != prev_b), 1, 0).astype(I32)
    efi = jnp.where(ev != prev_e, 1, 0).astype(I32)
    return pos, (bv, ev.astype(I32), lo, hi, fi, efi)


def _moe(routed, layer, wgu, bgu, wdn, bdn, g, b):
    h1, h1p, te, gates, rk, cnt = routed
    n_tok = h1.shape[0]
    pos, visits = _routing_tables(te[:, :TOP_K], rk[:, :TOP_K], cnt[0, :N_EXPERTS].astype(I32))
    xs = _sc_scatter(h1p, pos)
    y = _ffn(visits, xs, layer, wgu, bgu, wdn, bdn)
    yg = _sc_gather(y, jnp.transpose(pos).reshape(-1)).reshape(TOP_K, n_tok, HALF)
    return _combine(yg, h1, gates, g, b)


def _row(a):
    return a.reshape(1, -1).astype(F32)


def _even_layer(h2, bsz, seq, w_in, q_norm, kv_norm, w_uq, w_uk, w_uv, w_iq, idx_g, idx_b, v_g, v_b,
                w_s, b_s, w_out, ln_g, ln_b, rw, rb):
    o4 = D_CQ + D_C + D_IDX + H_IDX
    w_in_p = jnp.concatenate(
        [w_in[:, :o4], jnp.zeros((D_MODEL, E_IN_PAD - E_IN_EVEN), F32), w_in[:, o4:]], axis=1)
    pad_idx = lambda a: jnp.pad(_row(a), ((0, 0), (0, LANES - D_IDX)))
    cq, ckv, kidx, widx, uv = _proj_even(h2, w_in_p.astype(BF16), _row(q_norm), _row(kv_norm),
                                         pad_idx(idx_g), pad_idx(idx_b))
    as3 = lambda a: a.reshape(bsz, seq, a.shape[-1])

    n_sel = min(TOPK_MAX, seq // 4)
    tq = min(DSA_TQ, seq)
    wuq = jnp.transpose(w_uq, (1, 0, 2)).astype(BF16)
    wiq = jnp.pad(jnp.transpose(w_iq, (1, 0, 2)),
                  ((0, 0), (0, 0), (0, LANES - D_IDX))).astype(BF16)
    wuk = w_uk.astype(BF16)
    eye = jnp.eye(H_A, dtype=F32)
    wuv = (w_uv[:, :, None, :] * eye[:, None, :, None]).reshape(H_A, D_C, H_A * DH_A).astype(BF16)
    o_a = jnp.concatenate(
        [_dsa_tile(as3(cq), as3(ckv), as3(kidx), as3(widx), qt, tq, n_sel, wuq, wuk, wiq, wuv)
         for qt in range(seq // tq)], axis=1)

    gbias = jnp.repeat(jnp.transpose(b_s), CG_B, axis=1)
    o_b = _gmlp(as3(uv), _row(v_g), _row(v_b), w_s, gbias)

    n_a = H_A * DH_A
    w_out_b = w_out.astype(BF16)
    return _outln([o_a.reshape(bsz * seq, n_a), o_b.reshape(bsz * seq, GMLP_WIDTH)],
                  [w_out_b[:n_a], w_out_b[n_a:]], h2, _row(ln_g), _row(ln_b), rw, rb)


def _odd_layer(h2, bsz, seq, layer, w_in, lq1, lk1, lq2, lk2, subln_g, w_out, ln_g, ln_b, rw, rb):
    lam_init = 0.8 - 0.6 * math.exp(-0.3 * layer)
    qk, vt3 = _proj_odd(h2, w_in.astype(BF16), seq)
    qk3 = qk.reshape(bsz, seq, 2 * QK_W_C)
    tq = min(DIFF_TQ, seq)
    kpos = jnp.stack([_alibi_key_cols(seq, DH_C), _alibi_key_cols(seq, 0)])
    qa0 = _alibi_query_cols(H_C)
    qa = jnp.stack([jnp.roll(qa0, DH_C, axis=-1), qa0], axis=1)
    diag = jnp.transpose(_alibi_diag(H_C, tq), (0, 2, 1))
    g_col = subln_g.reshape(-1, 1).astype(F32)
    o = jnp.concatenate(
        [_diff_tile(qk3, vt3, qt, tq, lam_init, kpos, qa, diag, _row(lq1), _row(lk1), _row(lq2),
                    _row(lk2), g_col) for qt in range(seq // tq)], axis=1)
    return _outln([o.reshape(bsz * seq, QK_W_C)], [w_out.astype(BF16)], h2, _row(ln_g), _row(ln_b),
                  rw, rb)


def kernel(x, ev_w_in, ev_q_norm, ev_kv_norm, ev_w_uq, ev_w_uk, ev_w_uv, ev_w_iq, ev_idx_k_g, ev_idx_k_b, ev_v_norm_g, ev_v_norm_b, ev_w_s, ev_b_s, ev_w_out, od_w_in, od_lambda_q1, od_lambda_k1, od_lambda_q2, od_lambda_k2, od_subln_g, od_w_out, ln1_g, ln1_b, ln2_g, ln2_b, router_w, router_b, exp_w_gu, exp_b_gu, exp_w_dn, exp_b_dn):
    bsz, seq, d = x.shape
    gb = bsz // BATCH_GROUPS
    hs = [x[g * gb:(g + 1) * gb].reshape(gb * seq, d) for g in range(BATCH_GROUPS)]
    for l in range(DEPTH):
        j = l // 2
        rw = jnp.pad(router_w[l], ((0, 0), (0, LANES - N_EXPERTS)))
        rb = jnp.pad(router_b[l], (0, LANES - N_EXPERTS), constant_values=-jnp.inf).reshape(1, LANES)
        if l % 2 == 0:
            routed = [_even_layer(h2, gb, seq, ev_w_in[j], ev_q_norm[j], ev_kv_norm[j],
                                  ev_w_uq[j], ev_w_uk[j], ev_w_uv[j], ev_w_iq[j], ev_idx_k_g[j],
                                  ev_idx_k_b[j], ev_v_norm_g[j], ev_v_norm_b[j], ev_w_s[j],
                                  ev_b_s[j], ev_w_out[j], ln1_g[l], ln1_b[l], rw, rb) for h2 in hs]
        else:
            routed = [_odd_layer(h2, gb, seq, l, od_w_in[j], od_lambda_q1[j], od_lambda_k1[j],
                                 od_lambda_q2[j], od_lambda_k2[j], od_subln_g[j], od_w_out[j],
                                 ln1_g[l], ln1_b[l], rw, rb) for h2 in hs]
        hs = [_moe(r, l, exp_w_gu, exp_b_gu[:, :, None, :], exp_w_dn, exp_b_dn[:, :, None, :],
                   _row(ln2_g[l]), _row(ln2_b[l])) for r in routed]
    return jnp.concatenate(hs, axis=0).reshape(bsz, seq, d)
```

```python
import functools
import math

import jax
import jax.numpy as jnp
from jax import lax
from jax.experimental import pallas as pl
from jax.experimental.pallas import tpu as pltpu
from jax.experimental.pallas import tpu_sc as plsc

F32 = jnp.float32
BF16 = jnp.bfloat16
I32 = jnp.int32

D_MODEL = 1024
DEPTH = 4
CHUNK = 64
CHUNK_SHIFT = 6
H_A = 8
DH_A = 64
D_CQ = 256
D_C = 128
H_IDX = 4
D_IDX = 64
TOPK_MAX = 256
GMLP_CHUNK = 128
G_B = 8
GMLP_WIDTH = 512
CG_B = GMLP_WIDTH // G_B
H_C = 8
DH_C = 64
QK_W_C = H_C * 2 * DH_C
N_EXPERTS = 32
TOP_K = 4
D_FF = 1024
SWIGLU_ALPHA = 1.702
SWIGLU_LIMIT = 7.0
DN_ALPHA = (2 * DEPTH) ** 0.25
E_IN_EVEN = D_CQ + D_C + D_IDX + H_IDX + 2 * GMLP_WIDTH
LN_EPS = 1e-5
NEG = -1e30
LOG2E = 1.4426950408889634

LANES = 128
E_IN_PAD = 1536
INT_MIN = -(2 ** 31)
HALF = D_MODEL // 2
HI_MASK = -65536
SC_WIN = 64
POS_RADIX = 256
N_POS_COLS = 6

PROJ_TM = 512
DSA_TQ = 512
DIFF_TQ = 512
GMLP_TG = 512
FFN_TM = 512
FFN_FC = 512
COMB_TM = 512
VMEM_LIMIT = 56 * 1024 * 1024


def _cparams(sem):
    return pltpu.CompilerParams(dimension_semantics=sem, vmem_limit_bytes=VMEM_LIMIT)


def _full(a, n_grid):
    zeros = (0,) * a.ndim
    return pl.BlockSpec(a.shape, lambda *_: zeros)


def _alibi_slopes_l2(n):
    return jnp.exp2(-8.0 * jnp.arange(1, n + 1, dtype=F32) / n) * LOG2E


def _alibi_query_cols(n):
    s = _alibi_slopes_l2(n)
    hi = s.astype(BF16).astype(F32)
    mid = (s - hi).astype(BF16).astype(F32)
    lo = (s - hi - mid).astype(BF16).astype(F32)
    cols = jnp.stack([hi * POS_RADIX, hi, mid * POS_RADIX, mid, lo * POS_RADIX, lo], axis=-1)
    return jnp.pad(cols, ((0, 0), (0, LANES - N_POS_COLS))).reshape(n, 1, LANES)


def _alibi_key_cols(seq, lane0):
    s = jnp.arange(seq, dtype=I32)
    a = (s // POS_RADIX).astype(F32)
    b = (s % POS_RADIX).astype(F32)
    cols = jnp.stack([a, b, a, b, a, b], axis=-1)
    return jnp.pad(cols, ((0, 0), (lane0, LANES - N_POS_COLS - lane0))).astype(BF16)


def _alibi_diag(n, tq):
    t = jnp.arange(tq, dtype=I32)[:, None]
    s = jnp.arange(tq, dtype=I32)[None, :]
    ok = (s >> CHUNK_SHIFT) <= (t >> CHUNK_SHIFT)
    ahead = jnp.maximum(s - t, 0).astype(F32)
    corr = -2.0 * _alibi_slopes_l2(n)[:, None, None] * ahead[None]
    return jnp.where(ok[None], corr, NEG)


def _proj_even_kernel(x_ref, w_ref, qg_ref, kvg_ref, ig_ref, ib_ref,
                      cq_ref, ckv_ref, kidx_ref, widx_ref, uv_ref):
    acc = jnp.dot(x_ref[...].astype(BF16), w_ref[...], preferred_element_type=F32)
    cq = acc[:, :D_CQ]
    cq_ref[...] = (cq * lax.rsqrt(jnp.mean(cq * cq, axis=-1, keepdims=True) + LN_EPS)
                   * qg_ref[...]).astype(cq_ref.dtype)
    kv = acc[:, D_CQ:D_CQ + D_C]
    ckv_ref[...] = (kv * lax.rsqrt(jnp.mean(kv * kv, axis=-1, keepdims=True) + LN_EPS)
                    * kvg_ref[...]).astype(ckv_ref.dtype)
    blk = acc[:, D_CQ + D_C:D_CQ + D_C + LANES]
    lane = lax.broadcasted_iota(I32, blk.shape, 1)
    is_k = lane < D_IDX
    mu = jnp.sum(jnp.where(is_k, blk, 0.0), axis=-1, keepdims=True) * (1.0 / D_IDX)
    kc = jnp.where(is_k, blk - mu, 0.0)
    var = jnp.sum(kc * kc, axis=-1, keepdims=True) * (1.0 / D_IDX)
    kidx_ref[...] = jnp.where(is_k, kc * lax.rsqrt(var + LN_EPS) * ig_ref[...] + ib_ref[...],
                              0.0).astype(kidx_ref.dtype)
    widx_ref[...] = jnp.where(is_k, 0.0, blk * (H_IDX ** -0.5))
    uv_ref[...] = acc[:, E_IN_PAD - 2 * GMLP_WIDTH:]


def _proj_even(x2, w, qg, kvg, ig, ib):
    m, k = x2.shape
    tm = min(PROJ_TM, m)
    row = lambda n: pl.BlockSpec((tm, n), lambda i: (i, 0))
    consts = (qg, kvg, ig, ib)
    return pl.pallas_call(
        _proj_even_kernel,
        grid=(m // tm,),
        in_specs=[row(k), _full(w, 1)] + [_full(c, 1) for c in consts],
        out_specs=[row(D_CQ), row(D_C), row(LANES), row(LANES), row(2 * GMLP_WIDTH)],
        out_shape=[jax.ShapeDtypeStruct((m, D_CQ), BF16),
                   jax.ShapeDtypeStruct((m, D_C), BF16),
                   jax.ShapeDtypeStruct((m, LANES), BF16),
                   jax.ShapeDtypeStruct((m, LANES), F32),
                   jax.ShapeDtypeStruct((m, 2 * GMLP_WIDTH), F32)],
        compiler_params=_cparams(("parallel",)),
        name="proj_even",
    )(x2, w, *consts)


def _proj_odd_kernel(x_ref, w_ref, qk_ref, vt_ref):
    acc = jnp.dot(x_ref[...].astype(BF16), w_ref[...], preferred_element_type=F32)
    qk_ref[:, :QK_W_C] = (acc[:, :QK_W_C] * (DH_C ** -0.5 * LOG2E)).astype(qk_ref.dtype)
    qk_ref[:, QK_W_C:] = acc[:, QK_W_C:2 * QK_W_C].astype(qk_ref.dtype)
    w = 2 * DH_C
    for h in range(H_C):
        v = acc[:, 2 * QK_W_C + h * w:2 * QK_W_C + (h + 1) * w]
        vt_ref[h * w:(h + 1) * w, :] = v.T.astype(vt_ref.dtype)


def _proj_odd(x2, w, seq):
    m, k = x2.shape
    tm = min(PROJ_TM, seq)
    per_seq = seq // tm
    return pl.pallas_call(
        _proj_odd_kernel,
        grid=(m // tm,),
        in_specs=[pl.BlockSpec((tm, k), lambda i: (i, 0)), _full(w, 1)],
        out_specs=[pl.BlockSpec((tm, 2 * QK_W_C), lambda i: (i, 0)),
                   pl.BlockSpec((None, QK_W_C, tm), lambda i: (i // per_seq, 0, i % per_seq))],
        out_shape=[jax.ShapeDtypeStruct((m, 2 * QK_W_C), BF16),
                   jax.ShapeDtypeStruct((m // seq, QK_W_C, seq), BF16)],
        compiler_params=_cparams(("parallel",)),
        name="proj_odd",
    )(x2, w)


_NT = (((1,), (1,)), ((), ()))


COL_GROUPS = 8


def _col_reduce(x, red, comb):
    step = x.shape[0] // COL_GROUPS
    vals = [red(x[i * step:(i + 1) * step], axis=0, keepdims=True) for i in range(COL_GROUPS)]
    while len(vals) > 1:
        vals = [comb(vals[i], vals[i + 1]) for i in range(0, len(vals), 2)]
    return vals[0]


def _softmax_pv_t(parts, vt_parts):
    m = None
    for lg in parts:
        mx = _col_reduce(lg, jnp.max, jnp.maximum)
        m = mx if m is None else jnp.maximum(m, mx)
    l = None
    o = None
    for lg, vt in zip(parts, vt_parts):
        p = jnp.exp2(lg - m)
        s = _col_reduce(p, jnp.sum, jnp.add)
        d = jnp.dot(vt, p.astype(BF16), preferred_element_type=F32)
        l = s if l is None else l + s
        o = d if o is None else o + d
    return o * (1.0 / l)


def _softmax_pv(parts, kv_parts):
    m = None
    for lg in parts:
        mx = jnp.max(lg, axis=-1, keepdims=True)
        m = mx if m is None else jnp.maximum(m, mx)
    l = None
    o = None
    for lg, kv in zip(parts, kv_parts):
        p = jnp.exp2(lg - m)
        s = jnp.sum(p, axis=-1, keepdims=True)
        d = jnp.dot(p.astype(BF16), kv, preferred_element_type=F32)
        l = s if l is None else l + s
        o = d if o is None else o + d
    return o * (1.0 / l)


def _dsa_kernel(cq_ref, ckv_ref, kidx_ref, widx_ref, wuq_ref, wuk_ref, wiq_ref, wuv_ref, srow_ref,
                diag_ref, o_ref, bias_ref, *, q0, n_sel):
    tq = cq_ref.shape[0]
    sk = ckv_ref.shape[0]
    cqb = cq_ref[...]
    kidx = kidx_ref[...]
    widx = widx_ref[...]

    score = None
    for h in range(H_IDX):
        qi = jnp.dot(cqb, wiq_ref[h], preferred_element_type=F32).astype(BF16)
        r = lax.dot_general(qi, kidx, _NT, preferred_element_type=F32)
        term = jnp.maximum(r * (D_IDX ** -0.5), 0.0) * widx[:, D_IDX + h:D_IDX + h + 1]
        score = term if score is None else score + term
    t_pos = q0 + lax.broadcasted_iota(I32, (tq, sk), 0)
    s_pos = lax.broadcasted_iota(I32, (tq, sk), 1)
    allowed = (s_pos >> CHUNK_SHIFT) <= (t_pos >> CHUNK_SHIFT)
    score = jnp.where(allowed, score, NEG)

    bits = lax.bitcast_convert_type(score, I32)
    key = jnp.where(bits < 0, bits ^ 0x7FFFFFFF, bits)
    key = jnp.where(key == -1, 0, key)

    def count(mask):
        return jnp.sum(jnp.where(mask, 1.0, 0.0), axis=-1, keepdims=True)

    nsel = float(n_sel)
    cur = jnp.where(count(key >= 0) >= nsel, 0, INT_MIN).astype(I32)

    def search(i, cur):
        cand = cur | lax.shift_left(jnp.int32(1), 30 - i)
        return jnp.where(count(key >= cand) >= nsel, cand, cur)

    cur = lax.fori_loop(0, 31, search, cur)

    need = nsel - count(key > cur)
    ri = lax.broadcasted_iota(I32, (LANES, LANES), 0)
    ci = lax.broadcasted_iota(I32, (LANES, LANES), 1)
    tri = jnp.where(ri < ci, 1.0, 0.0).astype(BF16)
    ones_b = jnp.ones((LANES, LANES), BF16)
    off = jnp.zeros((tq, 1), F32)
    for c in range(sk // LANES):
        sl = slice(c * LANES, (c + 1) * LANES)
        keyc = key[:, sl]
        eqc = jnp.where(keyc == cur, 1.0, 0.0)
        eqb = eqc.astype(BF16)
        rank = jnp.dot(eqb, tri, preferred_element_type=F32) + off
        sel = jnp.where(keyc > cur, 1.0, jnp.where(rank < need, eqc, 0.0))
        bias_ref[:, sl] = jnp.where(sel > 0.5, 0.0, NEG)
        off = off + jnp.dot(eqb, ones_b, preferred_element_type=F32)[:, :1]

    n_lo = sk - tq
    kv_parts = ([ckv_ref[:n_lo, :]] if n_lo else []) + [ckv_ref[n_lo:, :]]
    acc = jnp.zeros((tq, H_A * DH_A), F32)
    for h in range(H_A):
        qh = jnp.dot(cqb, wuq_ref[h], preferred_element_type=F32).astype(BF16)
        ql = (jnp.dot(qh, wuk_ref[h], preferred_element_type=F32)
              * (DH_A ** -0.5 * LOG2E)).astype(BF16)
        parts = []
        if n_lo:
            lg = lax.dot_general(ql, kv_parts[0], _NT, preferred_element_type=F32)
            parts.append(lg + bias_ref[:, :n_lo] + srow_ref[h][:, :n_lo])
        lg = lax.dot_general(ql, kv_parts[-1], _NT, preferred_element_type=F32)
        parts.append(lg + bias_ref[:, n_lo:] + diag_ref[h])
        ol = _softmax_pv(parts, kv_parts)
        acc = acc + jnp.dot(ol.astype(BF16), wuv_ref[h], preferred_element_type=F32)
    o_ref[...] = acc.astype(o_ref.dtype)


def _dsa_tile(cq3, ckv3, kidx3, widx3, qt, tq, n_sel, wuq, wuk, wiq, wuv):
    b, s, _ = cq3.shape
    sk = (qt + 1) * tq
    slopes = _alibi_slopes_l2(H_A)
    srow = (slopes[:, None] * jnp.arange(sk, dtype=F32)[None, :]).reshape(H_A, 1, sk)
    diag = _alibi_diag(H_A, tq) + srow[:, :, sk - tq:]
    consts = (wuq, wuk, wiq, wuv, srow, diag)
    return pl.pallas_call(
        functools.partial(_dsa_kernel, q0=qt * tq, n_sel=n_sel),
        grid=(b,),
        in_specs=[pl.BlockSpec((None, tq, D_CQ), lambda i: (i, qt, 0)),
                  pl.BlockSpec((None, sk, D_C), lambda i: (i, 0, 0)),
                  pl.BlockSpec((None, sk, LANES), lambda i: (i, 0, 0)),
                  pl.BlockSpec((None, tq, LANES), lambda i: (i, qt, 0))]
        + [_full(c, 1) for c in consts],
        out_specs=pl.BlockSpec((None, tq, H_A * DH_A), lambda i: (i, 0, 0)),
        out_shape=jax.ShapeDtypeStruct((b, tq, H_A * DH_A), BF16),
        scratch_shapes=[pltpu.VMEM((tq, sk), F32)],
        compiler_params=_cparams(("parallel",)),
        name=f"dsa_q{qt}",
    )(cq3, ckv3, kidx3, widx3, *consts)


def _gelu(x):
    return 0.5 * x * (1.0 + jnp.tanh(0.7978845608028654 * (x + 0.044715 * (x * x * x))))


def _gmlp_kernel(u_ref, v_ref, vg_ref, vb_ref, ws_ref, bias_ref, o_ref):
    tg = u_ref.shape[0]
    ri = lax.broadcasted_iota(I32, (GMLP_CHUNK, GMLP_CHUNK), 0)
    ci = lax.broadcasted_iota(I32, (GMLP_CHUNK, GMLP_CHUNK), 1)
    tril = ri >= ci
    grp = lax.broadcasted_iota(I32, (GMLP_CHUNK, GMLP_WIDTH), 1) // CG_B
    ws = [jnp.where(tril, ws_ref[g], 0.0).astype(BF16) for g in range(G_B)]
    for c in range(tg // GMLP_CHUNK):
        sl = slice(c * GMLP_CHUNK, (c + 1) * GMLP_CHUNK)
        u = _gelu(u_ref[sl, :])
        v = _gelu(v_ref[sl, :])
        mu = jnp.mean(v, axis=-1, keepdims=True)
        vc = v - mu
        var = jnp.mean(vc * vc, axis=-1, keepdims=True)
        vn = (vc * lax.rsqrt(var + LN_EPS) * vg_ref[...] + vb_ref[...]).astype(BF16)
        mixed = bias_ref[...]
        for g in range(G_B):
            r = jnp.dot(ws[g], vn, preferred_element_type=F32)
            mixed = mixed + jnp.where(grp == g, r, 0.0)
        o_ref[sl, :] = (u * mixed).astype(o_ref.dtype)


def _gmlp(uv3, vg, vb, ws, bias):
    b, s, _ = uv3.shape
    tg = min(GMLP_TG, s)
    consts = (vg, vb, ws, bias)
    return pl.pallas_call(
        _gmlp_kernel,
        grid=(b, s // tg),
        in_specs=[pl.BlockSpec((None, tg, GMLP_WIDTH), lambda i, j: (i, j, 0)),
                  pl.BlockSpec((None, tg, GMLP_WIDTH), lambda i, j: (i, j, 1))]
        + [_full(c, 2) for c in consts],
        out_specs=pl.BlockSpec((None, tg, GMLP_WIDTH), lambda i, j: (i, j, 0)),
        out_shape=jax.ShapeDtypeStruct((b, s, GMLP_WIDTH), BF16),
        compiler_params=_cparams(("parallel", "parallel")),
        name="gmlp",
    )(uv3, uv3, *consts)


def _diff_kernel(q_ref, k_ref, vt_ref, kpos_ref, qa_ref, diag_ref, lq1_ref, lk1_ref, lq2_ref, lk2_ref,
                 g_ref, o_ref, *, lam_init):
    tq = q_ref.shape[0]
    sk = k_ref.shape[0]
    n_lo = sk - tq
    lam = (jnp.exp(jnp.sum(lq1_ref[...] * lk1_ref[...], axis=-1, keepdims=True))
           - jnp.exp(jnp.sum(lq2_ref[...] * lk2_ref[...], axis=-1, keepdims=True)) + lam_init)
    q = q_ref[...]
    k = k_ref[...]
    vt_parts = ([vt_ref[:, :n_lo]] if n_lo else []) + [vt_ref[:, n_lo:]]
    q_lane = lax.broadcasted_iota(I32, q.shape, 1)
    k_lane = lax.broadcasted_iota(I32, k.shape, 1)
    outs = []
    for m in range(2):
        q_own = (q_lane < DH_C) if m == 0 else (q_lane >= DH_C)
        k_own = (k_lane < DH_C) if m == 0 else (k_lane >= DH_C)
        qm = jnp.where(q_own, q, jnp.broadcast_to(qa_ref[m], q.shape).astype(BF16))
        km = jnp.where(k_own, k, kpos_ref[m])
        parts = []
        if n_lo:
            parts.append(lax.dot_general(km[:n_lo], qm, _NT, preferred_element_type=F32))
        lg = lax.dot_general(km[n_lo:], qm, _NT, preferred_element_type=F32)
        parts.append(lg + diag_ref[...])
        outs.append(_softmax_pv_t(parts, vt_parts))
    o = outs[0] - lam * outs[1]
    o = o * lax.rsqrt(jnp.mean(o * o, axis=0, keepdims=True) + LN_EPS) * g_ref[...]
    o_ref[...] = (o * (1.0 - lam_init)).T.astype(o_ref.dtype)


def _diff_tile(qk3, vt3, qt, tq, lam_init, kpos, qa, diag, lq1, lk1, lq2, lk2, g):
    b, s, _ = qk3.shape
    sk = (qt + 1) * tq
    w = 2 * DH_C
    consts = (lq1, lk1, lq2, lk2, g)
    return pl.pallas_call(
        functools.partial(_diff_kernel, lam_init=lam_init),
        grid=(b, H_C),
        in_specs=[pl.BlockSpec((None, tq, w), lambda i, j: (i, qt, j)),
                  pl.BlockSpec((None, sk, w), lambda i, j: (i, 0, H_C + j)),
                  pl.BlockSpec((None, w, sk), lambda i, j: (i, j, 0)),
                  pl.BlockSpec((2, sk, w), lambda i, j: (0, 0, 0)),
                  pl.BlockSpec((None, 2, 1, w), lambda i, j: (j, 0, 0, 0)),
                  pl.BlockSpec((None, tq, tq), lambda i, j: (j, 0, 0))]
        + [_full(c, 2) for c in consts],
        out_specs=pl.BlockSpec((None, tq, w), lambda i, j: (i, 0, j)),
        out_shape=jax.ShapeDtypeStruct((b, tq, H_C * w), BF16),
        compiler_params=_cparams(("parallel", "parallel")),
        name=f"diff_q{qt}",
    )(qk3, qk3, vt3, kpos, qa, diag, *consts)


def _pack_halves(x):
    lo = lax.bitcast_convert_type(x[:, :HALF].astype(BF16).astype(F32), I32)
    hi = lax.bitcast_convert_type(x[:, HALF:].astype(BF16).astype(F32), I32)
    return lax.shift_right_logical(lo, 16) | (hi & HI_MASK)


def _unpack_lo(w):
    return lax.bitcast_convert_type(lax.shift_left(w, 16), F32)


def _unpack_hi(w):
    return lax.bitcast_convert_type(w & HI_MASK, F32)


def _layer_norm_rows(z, g, b):
    mu = jnp.mean(z, axis=-1, keepdims=True)
    zc = z - mu
    var = jnp.mean(zc * zc, axis=-1, keepdims=True)
    return zc * lax.rsqrt(var + LN_EPS) * g + b


def _split2(x):
    hi = x.astype(BF16)
    mid = (x - hi.astype(F32)).astype(BF16)
    return hi, mid


def _outln_kernel(*refs, n_in):
    xs = refs[:n_in]
    ws = refs[n_in:2 * n_in]
    (h_ref, g_ref, b_ref, rw_ref, rb_ref, tri_ref,
     h1_ref, hp_ref, te_ref, tg_ref, rk_ref, cnt_ref) = refs[2 * n_in:]
    acc = None
    for x_ref, w_ref in zip(xs, ws):
        d = jnp.dot(x_ref[...], w_ref[...], preferred_element_type=F32)
        acc = d if acc is None else acc + d
    h1 = _layer_norm_rows(DN_ALPHA * h_ref[...] + acc, g_ref[...], b_ref[...])
    h1_ref[...] = h1
    hp_ref[...] = _pack_halves(h1)

    h_hi, h_mid = _split2(h1)
    dotf = lambda a, b: jnp.dot(a, b, preferred_element_type=F32)
    logits = dotf(h_hi, rw_ref[0]) + dotf(h_mid, rw_ref[0]) + dotf(h_hi, rw_ref[1]) + rb_ref[...]
    lane = lax.broadcasted_iota(I32, logits.shape, 1)
    vals, idxs = [], []
    for _ in range(TOP_K):
        mx = jnp.max(logits, axis=-1, keepdims=True)
        ix = jnp.min(jnp.where(logits == mx, lane, LANES), axis=-1, keepdims=True)
        vals.append(mx)
        idxs.append(ix)
        logits = jnp.where(lane == ix, -jnp.inf, logits)
    es = [jnp.exp(v - vals[0]) for v in vals]
    den = es[0] + es[1] + es[2] + es[3]

    @pl.when(pl.program_id(0) == 0)
    def _():
        cnt_ref[...] = jnp.zeros_like(cnt_ref)

    member = jnp.zeros(lane.shape, F32)
    for k in range(TOP_K):
        member = jnp.where(lane == idxs[k], 1.0, member)
    before = jnp.dot(tri_ref[...], member.astype(BF16), preferred_element_type=F32) + cnt_ref[...]
    te = jnp.zeros(lane.shape, I32)
    tg = jnp.zeros(lane.shape, F32)
    rk = jnp.zeros(lane.shape, F32)
    for k in range(TOP_K):
        rank_k = jnp.sum(jnp.where(lane == idxs[k], before, 0.0), axis=-1, keepdims=True)
        te = jnp.where(lane == k, idxs[k], te)
        tg = jnp.where(lane == k, es[k] / den, tg)
        rk = jnp.where(lane == k, rank_k, rk)
    te_ref[...] = te
    tg_ref[...] = tg
    rk_ref[...] = rk.astype(I32)
    cnt_ref[...] += jnp.sum(member, axis=0, keepdims=True)


def _outln(xs, ws, h2, g, b, rw, rb):
    m = h2.shape[0]
    tm = min(PROJ_TM, m)
    n_in = len(xs)
    row = lambda a: pl.BlockSpec((tm, a.shape[1]), lambda i: (i, 0))
    ri = lax.broadcasted_iota(I32, (tm, tm), 0)
    ci = lax.broadcasted_iota(I32, (tm, tm), 1)
    tri = (ci < ri).astype(BF16)
    rw3 = jnp.stack(_split2(rw))
    return pl.pallas_call(
        functools.partial(_outln_kernel, n_in=n_in),
        grid=(m // tm,),
        in_specs=[row(x) for x in xs] + [_full(w, 1) for w in ws]
        + [row(h2), _full(g, 1), _full(b, 1), _full(rw3, 1), _full(rb, 1), _full(tri, 1)],
        out_specs=[pl.BlockSpec((tm, D_MODEL), lambda i: (i, 0)),
                   pl.BlockSpec((tm, HALF), lambda i: (i, 0)),
                   pl.BlockSpec((tm, LANES), lambda i: (i, 0)),
                   pl.BlockSpec((tm, LANES), lambda i: (i, 0)),
                   pl.BlockSpec((tm, LANES), lambda i: (i, 0)),
                   pl.BlockSpec((1, LANES), lambda i: (0, 0))],
        out_shape=[jax.ShapeDtypeStruct((m, D_MODEL), F32),
                   jax.ShapeDtypeStruct((m, HALF), I32),
                   jax.ShapeDtypeStruct((m, LANES), I32),
                   jax.ShapeDtypeStruct((m, LANES), F32),
                   jax.ShapeDtypeStruct((m, LANES), I32),
                   jax.ShapeDtypeStruct((1, LANES), F32)],
        compiler_params=_cparams(("arbitrary",)),
        name="outln_router",
    )(*xs, *ws, h2, g, b, rw3, rb, tri)


def _sc_mesh():
    return plsc.VectorSubcoreMesh(core_axis_name="core", subcore_axis_name="subcore")


def _pad_windows(idx2):
    return jnp.pad(idx2, ((0, 0), (0, LANES - SC_WIN)))


def _sc_scatter(x, pos):
    t, d = x.shape
    nw = t // SC_WIN
    idx = _pad_windows(pos.reshape(nw, SC_WIN, TOP_K).transpose(0, 2, 1).reshape(nw * TOP_K, SC_WIN))

    @pl.kernel(out_type=jax.ShapeDtypeStruct((t * TOP_K, d), x.dtype), mesh=_sc_mesh())
    def scatter_rows(x_hbm, i_hbm, o_hbm):
        def body(x_vmem, i_vmem):
            pltpu.sync_copy(x_vmem, o_hbm.at[i_vmem.at[0, pl.ds(0, SC_WIN)]])

        pltpu.emit_pipeline(
            body,
            grid=(nw * TOP_K,),
            in_specs=[pl.BlockSpec((SC_WIN, d), lambda i: (i // TOP_K, 0)),
                      pl.BlockSpec((1, LANES), lambda i: (i, 0))],
            out_specs=[],
            core_axis_name=("core", "subcore"),
            dimension_semantics=(pltpu.PARALLEL,),
        )(x_hbm, i_hbm)

    return scatter_rows(x, idx)


def _sc_gather(y, idx):
    n = idx.shape[0]
    d = y.shape[1]
    idx2 = _pad_windows(idx.reshape(n // SC_WIN, SC_WIN))

    @pl.kernel(out_type=jax.ShapeDtypeStruct((n, d), y.dtype), mesh=_sc_mesh())
    def gather_rows(y_hbm, i_hbm, o_hbm):
        def body(i_vmem, o_vmem):
            pltpu.sync_copy(y_hbm.at[i_vmem.at[0, pl.ds(0, SC_WIN)]], o_vmem)

        pltpu.emit_pipeline(
            body,
            grid=(n // SC_WIN,),
            in_specs=[pl.BlockSpec((1, LANES), lambda i: (i, 0))],
            out_specs=[pl.BlockSpec((SC_WIN, d), lambda i: (i, 0))],
            core_axis_name=("core", "subcore"),
            dimension_semantics=(pltpu.PARALLEL,),
        )(i_hbm, o_hbm)

    return gather_rows(y, idx2)


def _ffn_kernel(vb_ref, ve_ref, lo_ref, hi_ref, first_ref, efirst_ref, x_ref, wgu_ref, bgu_ref,
                wdn_ref, bdn_ref, y_ref, wgu_s, wdn_s):
    v = pl.program_id(0)
    lo = lo_ref[v]
    hi = hi_ref[v]

    @pl.when(efirst_ref[v] == 1)
    def _():
        for c in range(D_MODEL // FFN_FC):
            rs = slice(c * FFN_FC, (c + 1) * FFN_FC)
            wgu_s[rs, :] = wgu_ref[rs, :].astype(BF16)
            wdn_s[rs, :] = wdn_ref[rs, :].astype(BF16)

    @pl.when(hi > lo)
    def _():
        xw = x_ref[...]
        xl = _unpack_lo(xw).astype(BF16)
        xh = _unpack_hi(xw).astype(BF16)
        acc = jnp.zeros((x_ref.shape[0], D_MODEL), F32)
        for c in range(D_FF // FFN_FC):
            gs = slice(c * FFN_FC, (c + 1) * FFN_FC)
            us = slice(D_FF + c * FFN_FC, D_FF + (c + 1) * FFN_FC)
            gate = (jnp.dot(xl, wgu_s[:HALF, gs], preferred_element_type=F32)
                    + jnp.dot(xh, wgu_s[HALF:, gs], preferred_element_type=F32) + bgu_ref[:, gs])
            up = (jnp.dot(xl, wgu_s[:HALF, us], preferred_element_type=F32)
                  + jnp.dot(xh, wgu_s[HALF:, us], preferred_element_type=F32) + bgu_ref[:, us])
            gate = jnp.minimum(gate, SWIGLU_LIMIT)
            up = jnp.clip(up, -SWIGLU_LIMIT, SWIGLU_LIMIT)
            act = (up + 1.0) * (gate * jax.nn.sigmoid(gate * SWIGLU_ALPHA))
            acc = acc + jnp.dot(act.astype(BF16), wdn_s[gs, :], preferred_element_type=F32)
        new = _pack_halves(acc + bdn_ref[...])
        row = lax.broadcasted_iota(I32, new.shape, 0)
        keep = jnp.where(first_ref[v] == 1, jnp.zeros_like(new), y_ref[...])
        y_ref[...] = jnp.where(row >= lo, jnp.where(row < hi, new, keep), keep)


def _ffn(visits, xs, layer, wgu, bgu, wdn, bdn):
    n_rows, hw = xs.shape
    n_vis = visits[0].shape[0]
    blk = lambda v, vb, *_: (vb[v], 0)
    exp = lambda v, vb, ve, *_: (layer, ve[v], 0, 0)
    grid_spec = pltpu.PrefetchScalarGridSpec(
        num_scalar_prefetch=len(visits),
        grid=(n_vis,),
        in_specs=[pl.BlockSpec((FFN_TM, hw), blk),
                  pl.BlockSpec((None, None, D_MODEL, 2 * D_FF), exp),
                  pl.BlockSpec((None, None, 1, 2 * D_FF), exp),
                  pl.BlockSpec((None, None, D_FF, D_MODEL), exp),
                  pl.BlockSpec((None, None, 1, D_MODEL), exp)],
        out_specs=pl.BlockSpec((FFN_TM, hw), blk),
        scratch_shapes=[pltpu.VMEM((D_MODEL, 2 * D_FF), BF16), pltpu.VMEM((D_FF, D_MODEL), BF16)],
    )
    return pl.pallas_call(
        _ffn_kernel,
        grid_spec=grid_spec,
        out_shape=jax.ShapeDtypeStruct((n_rows, hw), I32),
        compiler_params=_cparams(("arbitrary",)),
        name="moe_ffn",
    )(*visits, xs, wgu, bgu, wdn, bdn)


def _combine_kernel(yg_ref, h_ref, gate_ref, g_ref, b_ref, o_ref):
    gate = gate_ref[...]
    h = h_ref[...]
    zl = DN_ALPHA * h[:, :HALF]
    zh = DN_ALPHA * h[:, HALF:]
    for k in range(TOP_K):
        w = yg_ref[k]
        gk = gate[:, k:k + 1]
        zl = zl + gk * _unpack_lo(w)
        zh = zh + gk * _unpack_hi(w)
    mu = (jnp.sum(zl, axis=-1, keepdims=True) + jnp.sum(zh, axis=-1, keepdims=True)) * (1.0 / D_MODEL)
    zl = zl - mu
    zh = zh - mu
    var = (jnp.sum(zl * zl, axis=-1, keepdims=True)
           + jnp.sum(zh * zh, axis=-1, keepdims=True)) * (1.0 / D_MODEL)
    r = lax.rsqrt(var + LN_EPS)
    o_ref[:, :HALF] = zl * r * g_ref[:, :HALF] + b_ref[:, :HALF]
    o_ref[:, HALF:] = zh * r * g_ref[:, HALF:] + b_ref[:, HALF:]


def _combine(yg, h1, gates, g, b):
    m, d = h1.shape
    tm = min(COMB_TM, m)
    return pl.pallas_call(
        _combine_kernel,
        grid=(m // tm,),
        in_specs=[pl.BlockSpec((TOP_K, tm, HALF), lambda i: (0, i, 0)),
                  pl.BlockSpec((tm, d), lambda i: (i, 0)),
                  pl.BlockSpec((tm, LANES), lambda i: (i, 0)),
                  _full(g, 1), _full(b, 1)],
        out_specs=pl.BlockSpec((tm, d), lambda i: (i, 0)),
        out_shape=jax.ShapeDtypeStruct((m, d), F32),
        compiler_params=_cparams(("parallel",)),
        name="moe_combine",
    )(yg, h1, gates, g, b)


def _routing_tables(top_e, rank, counts):
    n_tok = top_e.shape[0]
    n_blocks = n_tok * TOP_K // FFN_TM
    n_vis = n_blocks + N_EXPERTS - 1
    end = jnp.cumsum(counts)
    start = end - counts
    onehot = top_e[:, :, None] == jnp.arange(N_EXPERTS, dtype=I32)[None, None, :]
    pos = (rank + jnp.sum(jnp.where(onehot, start[None, None, :], 0), axis=-1)).astype(I32)

    first = start // FFN_TM
    last = jnp.maximum(end - 1, 0) // FFN_TM
    nvis = jnp.where(counts > 0, last - first + 1, 0)
    vend = jnp.cumsum(nvis)
    vstart = vend - nvis
    total = vend[-1]
    v = jnp.arange(n_vis, dtype=I32)
    valid = v < total
    ev = jnp.sum((vend[None, :] <= jnp.minimum(v, total - 1)[:, None]).astype(I32), axis=1)
    ev = jnp.minimum(ev, N_EXPERTS - 1)
    pick = lambda tbl: jnp.sum(jnp.where(ev[:, None] == jnp.arange(N_EXPERTS, dtype=I32)[None, :],
                                         tbl[None, :], 0), axis=1)
    bv = jnp.where(valid, pick(first) + v - pick(vstart), n_blocks - 1).astype(I32)
    lo = jnp.where(valid, jnp.maximum(pick(start), bv * FFN_TM) - bv * FFN_TM, 0).astype(I32)
    hi = jnp.where(valid, jnp.minimum(pick(end), (bv + 1) * FFN_TM) - bv * FFN_TM, 0).astype(I32)
    prev_b = jnp.concatenate([jnp.full((1,), -1, I32), bv[:-1]])
    prev_e = jnp.concatenate([jnp.full((1,), -1, I32), ev[:-1]])
    fi = jnp.where(valid & (bv != prev_b), 1, 0).astype(I32)
    efi = jnp.where(ev != prev_e, 1, 0).astype(I32)
    return pos, (bv, ev.astype(I32), lo, hi, fi, efi)


def _moe(routed, layer, wgu, bgu, wdn, bdn, g, b):
    h1, h1p, te, gates, rk, cnt = routed
    n_tok = h1.shape[0]
    pos, visits = _routing_tables(te[:, :TOP_K], rk[:, :TOP_K], cnt[0, :N_EXPERTS].astype(I32))
    xs = _sc_scatter(h1p, pos)
    y = _ffn(visits, xs, layer, wgu, bgu, wdn, bdn)
    yg = _sc_gather(y, jnp.transpose(pos).reshape(-1)).reshape(TOP_K, n_tok, HALF)
    return _combine(yg, h1, gates, g, b)


def _row(a):
    return a.reshape(1, -1).astype(F32)


def _even_layer(h2, bsz, seq, w_in, q_norm, kv_norm, w_uq, w_uk, w_uv, w_iq, idx_g, idx_b, v_g, v_b,
                w_s, b_s, w_out, ln_g, ln_b, rw, rb):
    o4 = D_CQ + D_C + D_IDX + H_IDX
    w_in_p = jnp.concatenate(
        [w_in[:, :o4], jnp.zeros((D_MODEL, E_IN_PAD - E_IN_EVEN), F32), w_in[:, o4:]], axis=1)
    pad_idx = lambda a: jnp.pad(_row(a), ((0, 0), (0, LANES - D_IDX)))
    cq, ckv, kidx, widx, uv = _proj_even(h2, w_in_p.astype(BF16), _row(q_norm), _row(kv_norm),
                                         pad_idx(idx_g), pad_idx(idx_b))
    as3 = lambda a: a.reshape(bsz, seq, a.shape[-1])

    n_sel = min(TOPK_MAX, seq // 4)
    tq = min(DSA_TQ, seq)
    wuq = jnp.transpose(w_uq, (1, 0, 2)).astype(BF16)
    wiq = jnp.pad(jnp.transpose(w_iq, (1, 0, 2)),
                  ((0, 0), (0, 0), (0, LANES - D_IDX))).astype(BF16)
    wuk = w_uk.astype(BF16)
    eye = jnp.eye(H_A, dtype=F32)
    wuv = (w_uv[:, :, None, :] * eye[:, None, :, None]).reshape(H_A, D_C, H_A * DH_A).astype(BF16)
    o_a = jnp.concatenate(
        [_dsa_tile(as3(cq), as3(ckv), as3(kidx), as3(widx), qt, tq, n_sel, wuq, wuk, wiq, wuv)
         for qt in range(seq // tq)], axis=1)

    gbias = jnp.repeat(jnp.transpose(b_s), CG_B, axis=1)
    o_b = _gmlp(as3(uv), _row(v_g), _row(v_b), w_s, gbias)

    n_a = H_A * DH_A
    w_out_b = w_out.astype(BF16)
    return _outln([o_a.reshape(bsz * seq, n_a), o_b.reshape(bsz * seq, GMLP_WIDTH)],
                  [w_out_b[:n_a], w_out_b[n_a:]], h2, _row(ln_g), _row(ln_b), rw, rb)


def _odd_layer(h2, bsz, seq, layer, w_in, lq1, lk1, lq2, lk2, subln_g, w_out, ln_g, ln_b, rw, rb):
    lam_init = 0.8 - 0.6 * math.exp(-0.3 * layer)
    qk, vt3 = _proj_odd(h2, w_in.astype(BF16), seq)
    qk3 = qk.reshape(bsz, seq, 2 * QK_W_C)
    tq = min(DIFF_TQ, seq)
    kpos = jnp.stack([_alibi_key_cols(seq, DH_C), _alibi_key_cols(seq, 0)])
    qa0 = _alibi_query_cols(H_C)
    qa = jnp.stack([jnp.roll(qa0, DH_C, axis=-1), qa0], axis=1)
    diag = jnp.transpose(_alibi_diag(H_C, tq), (0, 2, 1))
    g_col = subln_g.reshape(-1, 1).astype(F32)
    o = jnp.concatenate(
        [_diff_tile(qk3, vt3, qt, tq, lam_init, kpos, qa, diag, _row(lq1), _row(lk1), _row(lq2),
                    _row(lk2), g_col) for qt in range(seq // tq)], axis=1)
    return _outln([o.reshape(bsz * seq, QK_W_C)], [w_out.astype(BF16)], h2, _row(ln_g), _row(ln_b),
                  rw, rb)


def kernel(x, ev_w_in, ev_q_norm, ev_kv_norm, ev_w_uq, ev_w_uk, ev_w_uv, ev_w_iq, ev_idx_k_g, ev_idx_k_b, ev_v_norm_g, ev_v_norm_b, ev_w_s, ev_b_s, ev_w_out, od_w_in, od_lambda_q1, od_lambda_k1, od_lambda_q2, od_lambda_k2, od_subln_g, od_w_out, ln1_g, ln1_b, ln2_g, ln2_b, router_w, router_b, exp_w_gu, exp_b_gu, exp_w_dn, exp_b_dn):
    bsz, seq, d = x.shape
    h2 = x.reshape(bsz * seq, d)
    for l in range(DEPTH):
        j = l // 2
        rw = jnp.pad(router_w[l], ((0, 0), (0, LANES - N_EXPERTS)))
        rb = jnp.pad(router_b[l], (0, LANES - N_EXPERTS), constant_values=-jnp.inf).reshape(1, LANES)
        if l % 2 == 0:
            routed = _even_layer(h2, bsz, seq, ev_w_in[j], ev_q_norm[j], ev_kv_norm[j],
                                 ev_w_uq[j], ev_w_uk[j], ev_w_uv[j], ev_w_iq[j], ev_idx_k_g[j],
                                 ev_idx_k_b[j], ev_v_norm_g[j], ev_v_norm_b[j], ev_w_s[j],
                                 ev_b_s[j], ev_w_out[j], ln1_g[l], ln1_b[l], rw, rb)
        else:
            routed = _odd_layer(h2, bsz, seq, l, od_w_in[j], od_lambda_q1[j], od_lambda_k1[j],
                                od_lambda_q2[j], od_lambda_k2[j], od_subln_g[j], od_w_out[j],
                                ln1_g[l], ln1_b[l], rw, rb)
        h2 = _moe(routed, l, exp_w_gu, exp_b_gu[:, :, None, :], exp_w_dn, exp_b_dn[:, :, None, :],
                  _row(ln2_g[l]), _row(ln2_b[l]))
    return h2.reshape(bsz, seq, d)
```

```python
import functools
import math

import jax
import jax.numpy as jnp
from jax import lax
from jax.experimental import pallas as pl
from jax.experimental.pallas import tpu as pltpu
from jax.experimental.pallas import tpu_sc as plsc

F32 = jnp.float32
BF16 = jnp.bfloat16
I32 = jnp.int32

D_MODEL = 1024
DEPTH = 4
CHUNK = 64
CHUNK_SHIFT = 6
H_A = 8
DH_A = 64
D_CQ = 256
D_C = 128
H_IDX = 4
D_IDX = 64
TOPK_MAX = 256
GMLP_CHUNK = 128
G_B = 8
GMLP_WIDTH = 512
CG_B = GMLP_WIDTH // G_B
H_C = 8
DH_C = 64
QK_W_C = H_C * 2 * DH_C
N_EXPERTS = 32
TOP_K = 4
D_FF = 1024
SWIGLU_ALPHA = 1.702
SWIGLU_LIMIT = 7.0
DN_ALPHA = (2 * DEPTH) ** 0.25
E_IN_EVEN = D_CQ + D_C + D_IDX + H_IDX + 2 * GMLP_WIDTH
LN_EPS = 1e-5
NEG = -1e30
LOG2E = 1.4426950408889634

LANES = 128
E_IN_PAD = 1536
INT_MIN = -(2 ** 31)
SIGN_FLIP = 0x7FFFFFFF
KEY_BITS = 32
KEY_BLOCK = 256
HALF = D_MODEL // 2
HI_MASK = -65536
SC_WIN = 64
POS_RADIX = 256
N_POS_COLS = 6

PROJ_TM = 512
DSA_TQ = 512
DIFF_TQ = 512
GMLP_TG = 512
FFN_TM = 512
FFN_FC = 512
COMB_TM = 512
VMEM_LIMIT = 56 * 1024 * 1024


def _cparams(sem):
    return pltpu.CompilerParams(dimension_semantics=sem, vmem_limit_bytes=VMEM_LIMIT)


def _full(a):
    zeros = (0,) * a.ndim
    return pl.BlockSpec(a.shape, lambda *_: zeros)


def _alibi_slopes_l2(n):
    return jnp.exp2(-8.0 * jnp.arange(1, n + 1, dtype=F32) / n) * LOG2E


def _alibi_query_cols(n):
    s = _alibi_slopes_l2(n)
    hi = s.astype(BF16).astype(F32)
    mid = (s - hi).astype(BF16).astype(F32)
    lo = (s - hi - mid).astype(BF16).astype(F32)
    cols = jnp.stack([hi * POS_RADIX, hi, mid * POS_RADIX, mid, lo * POS_RADIX, lo], axis=-1)
    return jnp.pad(cols, ((0, 0), (0, LANES - N_POS_COLS))).reshape(n, 1, LANES)


def _alibi_key_cols(seq, lane0):
    s = jnp.arange(seq, dtype=I32)
    a = (s // POS_RADIX).astype(F32)
    b = (s % POS_RADIX).astype(F32)
    cols = jnp.stack([a, b, a, b, a, b], axis=-1)
    return jnp.pad(cols, ((0, 0), (lane0, LANES - N_POS_COLS - lane0))).astype(BF16)


def _alibi_diag(n, tq):
    t = jnp.arange(tq, dtype=I32)[:, None]
    s = jnp.arange(tq, dtype=I32)[None, :]
    ok = (s >> CHUNK_SHIFT) <= (t >> CHUNK_SHIFT)
    ahead = jnp.maximum(s - t, 0).astype(F32)
    corr = -2.0 * _alibi_slopes_l2(n)[:, None, None] * ahead[None]
    return jnp.where(ok[None], corr, NEG)


def _proj_even_kernel(x_ref, w_ref, qg_ref, kvg_ref, ig_ref, ib_ref,
                      cq_ref, ckv_ref, kidx_ref, widx_ref, uv_ref):
    acc = jnp.dot(x_ref[...].astype(BF16), w_ref[...], preferred_element_type=F32)
    cq = acc[:, :D_CQ]
    cq_ref[...] = (cq * lax.rsqrt(jnp.mean(cq * cq, axis=-1, keepdims=True) + LN_EPS)
                   * qg_ref[...]).astype(cq_ref.dtype)
    kv = acc[:, D_CQ:D_CQ + D_C]
    ckv_ref[...] = (kv * lax.rsqrt(jnp.mean(kv * kv, axis=-1, keepdims=True) + LN_EPS)
                    * kvg_ref[...]).astype(ckv_ref.dtype)
    blk = acc[:, D_CQ + D_C:D_CQ + D_C + LANES]
    lane = lax.broadcasted_iota(I32, blk.shape, 1)
    is_k = lane < D_IDX
    mu = jnp.sum(jnp.where(is_k, blk, 0.0), axis=-1, keepdims=True) * (1.0 / D_IDX)
    kc = jnp.where(is_k, blk - mu, 0.0)
    var = jnp.sum(kc * kc, axis=-1, keepdims=True) * (1.0 / D_IDX)
    kidx_ref[...] = jnp.where(is_k, kc * lax.rsqrt(var + LN_EPS) * ig_ref[...] + ib_ref[...],
                              0.0).astype(kidx_ref.dtype)
    widx_ref[...] = jnp.where(is_k, 0.0, blk * (H_IDX ** -0.5))
    uv_ref[...] = acc[:, E_IN_PAD - 2 * GMLP_WIDTH:]


def _proj_even(x2, w, qg, kvg, ig, ib):
    m, k = x2.shape
    tm = min(PROJ_TM, m)
    row = lambda n: pl.BlockSpec((tm, n), lambda i: (i, 0))
    consts = (qg, kvg, ig, ib)
    return pl.pallas_call(
        _proj_even_kernel,
        grid=(m // tm,),
        in_specs=[row(k), _full(w)] + [_full(c) for c in consts],
        out_specs=[row(D_CQ), row(D_C), row(LANES), row(LANES), row(2 * GMLP_WIDTH)],
        out_shape=[jax.ShapeDtypeStruct((m, D_CQ), BF16),
                   jax.ShapeDtypeStruct((m, D_C), BF16),
                   jax.ShapeDtypeStruct((m, LANES), BF16),
                   jax.ShapeDtypeStruct((m, LANES), F32),
                   jax.ShapeDtypeStruct((m, 2 * GMLP_WIDTH), F32)],
        compiler_params=_cparams(("parallel",)),
        name="proj_even",
    )(x2, w, *consts)


def _proj_odd_kernel(x_ref, w_ref, qk_ref, vt_ref):
    acc = jnp.dot(x_ref[...].astype(BF16), w_ref[...], preferred_element_type=F32)
    qk_ref[:, :QK_W_C] = (acc[:, :QK_W_C] * (DH_C ** -0.5 * LOG2E)).astype(qk_ref.dtype)
    qk_ref[:, QK_W_C:] = acc[:, QK_W_C:2 * QK_W_C].astype(qk_ref.dtype)
    w = 2 * DH_C
    for h in range(H_C):
        v = acc[:, 2 * QK_W_C + h * w:2 * QK_W_C + (h + 1) * w]
        vt_ref[h * w:(h + 1) * w, :] = v.T.astype(vt_ref.dtype)


def _proj_odd(x2, w, seq):
    m, k = x2.shape
    tm = min(PROJ_TM, seq)
    per_seq = seq // tm
    return pl.pallas_call(
        _proj_odd_kernel,
        grid=(m // tm,),
        in_specs=[pl.BlockSpec((tm, k), lambda i: (i, 0)), _full(w)],
        out_specs=[pl.BlockSpec((tm, 2 * QK_W_C), lambda i: (i, 0)),
                   pl.BlockSpec((None, QK_W_C, tm), lambda i: (i // per_seq, 0, i % per_seq))],
        out_shape=[jax.ShapeDtypeStruct((m, 2 * QK_W_C), BF16),
                   jax.ShapeDtypeStruct((m // seq, QK_W_C, seq), BF16)],
        compiler_params=_cparams(("parallel",)),
        name="proj_odd",
    )(x2, w)


_NT = (((1,), (1,)), ((), ()))


COL_GROUPS = 8


def _col_reduce(x, red, comb):
    step = x.shape[0] // COL_GROUPS
    vals = [red(x[i * step:(i + 1) * step], axis=0, keepdims=True) for i in range(COL_GROUPS)]
    while len(vals) > 1:
        vals = [comb(vals[i], vals[i + 1]) for i in range(0, len(vals), 2)]
    return vals[0]


def _softmax_pv_t(parts, vt_parts):
    m = None
    for lg in parts:
        mx = _col_reduce(lg, jnp.max, jnp.maximum)
        m = mx if m is None else jnp.maximum(m, mx)
    l = None
    o = None
    for lg, vt in zip(parts, vt_parts):
        p = jnp.exp2(lg - m)
        s = _col_reduce(p, jnp.sum, jnp.add)
        d = jnp.dot(vt, p.astype(BF16), preferred_element_type=F32)
        l = s if l is None else l + s
        o = d if o is None else o + d
    return o * (1.0 / l)


def _softmax_pv(parts, kv_parts):
    m = None
    for lg in parts:
        mx = jnp.max(lg, axis=-1, keepdims=True)
        m = mx if m is None else jnp.maximum(m, mx)
    l = None
    o = None
    for lg, kv in zip(parts, kv_parts):
        p = jnp.exp2(lg - m)
        s = jnp.sum(p, axis=-1, keepdims=True)
        d = jnp.dot(p.astype(BF16), kv, preferred_element_type=F32)
        l = s if l is None else l + s
        o = d if o is None else o + d
    return o * (1.0 / l)


def _dsa_kernel(cq_ref, ckv_ref, kidx_ref, widx_ref, wuq_ref, wuk_ref, wiq_ref, wuv_ref, srow_ref,
                diag_ref, o_ref, bias_ref, key_ref, *, q0, n_sel):
    tq = cq_ref.shape[0]
    sk = ckv_ref.shape[0]
    cqb = cq_ref[...]
    widx = widx_ref[...]

    qis = [jnp.dot(cqb, wiq_ref[h], preferred_element_type=F32).astype(BF16) for h in range(H_IDX)]
    for c in range(sk // KEY_BLOCK):
        sl = slice(c * KEY_BLOCK, (c + 1) * KEY_BLOCK)
        score = None
        for h in range(H_IDX):
            r = lax.dot_general(qis[h], kidx_ref[sl, :], _NT, preferred_element_type=F32)
            term = jnp.maximum(r * (D_IDX ** -0.5), 0.0) * widx[:, D_IDX + h:D_IDX + h + 1]
            score = term if score is None else score + term
        if (c + 1) * KEY_BLOCK > sk - tq:
            t_pos = q0 + lax.broadcasted_iota(I32, (tq, KEY_BLOCK), 0)
            s_pos = c * KEY_BLOCK + lax.broadcasted_iota(I32, (tq, KEY_BLOCK), 1)
            score = jnp.where((s_pos >> CHUNK_SHIFT) <= (t_pos >> CHUNK_SHIFT), score, NEG)
        bits = lax.bitcast_convert_type(score, I32)
        kb = jnp.where(bits < 0, bits ^ SIGN_FLIP, bits)
        key_ref[:, sl] = jnp.where(kb == -1, 0, kb)

    def count(mask):
        return jnp.sum(jnp.where(mask, 1.0, 0.0), axis=-1, keepdims=True)

    nsel = float(n_sel)
    cur = jnp.where(count(key_ref[...] >= 0) >= nsel, 0, INT_MIN).astype(I32)

    def search(i, cur):
        cand = cur | lax.shift_left(jnp.int32(1), KEY_BITS - 2 - i)
        return jnp.where(count(key_ref[...] >= cand) >= nsel, cand, cur)

    cur = lax.fori_loop(0, KEY_BITS - 1, search, cur)

    need = nsel - count(key_ref[...] > cur)
    ri = lax.broadcasted_iota(I32, (LANES, LANES), 0)
    ci = lax.broadcasted_iota(I32, (LANES, LANES), 1)
    tri = jnp.where(ri < ci, 1.0, 0.0).astype(BF16)
    ones_b = jnp.ones((LANES, LANES), BF16)
    off = jnp.zeros((tq, 1), F32)
    for c in range(sk // LANES):
        sl = slice(c * LANES, (c + 1) * LANES)
        keyc = key_ref[:, sl]
        eqc = jnp.where(keyc == cur, 1.0, 0.0)
        eqb = eqc.astype(BF16)
        rank = jnp.dot(eqb, tri, preferred_element_type=F32) + off
        sel = jnp.where(keyc > cur, 1.0, jnp.where(rank < need, eqc, 0.0))
        bias_ref[:, sl] = jnp.where(sel > 0.5, 0.0, NEG)
        off = off + jnp.dot(eqb, ones_b, preferred_element_type=F32)[:, :1]

    n_lo = sk - tq
    kv_parts = ([ckv_ref[:n_lo, :]] if n_lo else []) + [ckv_ref[n_lo:, :]]
    acc = jnp.zeros((tq, H_A * DH_A), F32)
    for h in range(H_A):
        qh = jnp.dot(cqb, wuq_ref[h], preferred_element_type=F32).astype(BF16)
        ql = (jnp.dot(qh, wuk_ref[h], preferred_element_type=F32)
              * (DH_A ** -0.5 * LOG2E)).astype(BF16)
        parts = []
        if n_lo:
            lg = lax.dot_general(ql, kv_parts[0], _NT, preferred_element_type=F32)
            parts.append(lg + bias_ref[:, :n_lo] + srow_ref[h][:, :n_lo])
        lg = lax.dot_general(ql, kv_parts[-1], _NT, preferred_element_type=F32)
        parts.append(lg + bias_ref[:, n_lo:] + diag_ref[h])
        ol = _softmax_pv(parts, kv_parts)
        acc = acc + jnp.dot(ol.astype(BF16), wuv_ref[h], preferred_element_type=F32)
    o_ref[...] = acc.astype(o_ref.dtype)


def _dsa_tile(cq3, ckv3, kidx3, widx3, qt, tq, n_sel, wuq, wuk, wiq, wuv):
    b, s, _ = cq3.shape
    sk = (qt + 1) * tq
    slopes = _alibi_slopes_l2(H_A)
    srow = (slopes[:, None] * jnp.arange(sk, dtype=F32)[None, :]).reshape(H_A, 1, sk)
    diag = _alibi_diag(H_A, tq) + srow[:, :, sk - tq:]
    consts = (wuq, wuk, wiq, wuv, srow, diag)
    return pl.pallas_call(
        functools.partial(_dsa_kernel, q0=qt * tq, n_sel=n_sel),
        grid=(b,),
        in_specs=[pl.BlockSpec((None, tq, D_CQ), lambda i: (i, qt, 0)),
                  pl.BlockSpec((None, sk, D_C), lambda i: (i, 0, 0)),
                  pl.BlockSpec((None, sk, LANES), lambda i: (i, 0, 0)),
                  pl.BlockSpec((None, tq, LANES), lambda i: (i, qt, 0))]
        + [_full(c) for c in consts],
        out_specs=pl.BlockSpec((None, tq, H_A * DH_A), lambda i: (i, 0, 0)),
        out_shape=jax.ShapeDtypeStruct((b, tq, H_A * DH_A), BF16),
        scratch_shapes=[pltpu.VMEM((tq, sk), F32), pltpu.VMEM((tq, sk), I32)],
        compiler_params=_cparams(("parallel",)),
        name=f"dsa_q{qt}",
    )(cq3, ckv3, kidx3, widx3, *consts)


def _gelu(x):
    return 0.5 * x * (1.0 + jnp.tanh(0.7978845608028654 * (x + 0.044715 * (x * x * x))))


def _gmlp_kernel(u_ref, v_ref, vg_ref, vb_ref, ws_ref, bias_ref, o_ref):
    tg = u_ref.shape[0]
    ri = lax.broadcasted_iota(I32, (GMLP_CHUNK, GMLP_CHUNK), 0)
    ci = lax.broadcasted_iota(I32, (GMLP_CHUNK, GMLP_CHUNK), 1)
    tril = ri >= ci
    grp = lax.broadcasted_iota(I32, (GMLP_CHUNK, GMLP_WIDTH), 1) // CG_B
    ws = [jnp.where(tril, ws_ref[g], 0.0).astype(BF16) for g in range(G_B)]
    for c in range(tg // GMLP_CHUNK):
        sl = slice(c * GMLP_CHUNK, (c + 1) * GMLP_CHUNK)
        u = _gelu(u_ref[sl, :])
        v = _gelu(v_ref[sl, :])
        mu = jnp.mean(v, axis=-1, keepdims=True)
        vc = v - mu
        var = jnp.mean(vc * vc, axis=-1, keepdims=True)
        vn = (vc * lax.rsqrt(var + LN_EPS) * vg_ref[...] + vb_ref[...]).astype(BF16)
        mixed = bias_ref[...]
        for g in range(G_B):
            r = jnp.dot(ws[g], vn, preferred_element_type=F32)
            mixed = mixed + jnp.where(grp == g, r, 0.0)
        o_ref[sl, :] = (u * mixed).astype(o_ref.dtype)


def _gmlp(uv3, vg, vb, ws, bias):
    b, s, _ = uv3.shape
    tg = min(GMLP_TG, s)
    consts = (vg, vb, ws, bias)
    return pl.pallas_call(
        _gmlp_kernel,
        grid=(b, s // tg),
        in_specs=[pl.BlockSpec((None, tg, GMLP_WIDTH), lambda i, j: (i, j, 0)),
                  pl.BlockSpec((None, tg, GMLP_WIDTH), lambda i, j: (i, j, 1))]
        + [_full(c) for c in consts],
        out_specs=pl.BlockSpec((None, tg, GMLP_WIDTH), lambda i, j: (i, j, 0)),
        out_shape=jax.ShapeDtypeStruct((b, s, GMLP_WIDTH), BF16),
        compiler_params=_cparams(("parallel", "parallel")),
        name="gmlp",
    )(uv3, uv3, *consts)


def _diff_kernel(q_ref, k_ref, vt_ref, kpos_ref, qa_ref, diag_ref, lq1_ref, lk1_ref, lq2_ref, lk2_ref,
                 g_ref, o_ref, *, lam_init):
    tq = q_ref.shape[0]
    sk = k_ref.shape[0]
    n_lo = sk - tq
    lam = (jnp.exp(jnp.sum(lq1_ref[...] * lk1_ref[...], axis=-1, keepdims=True))
           - jnp.exp(jnp.sum(lq2_ref[...] * lk2_ref[...], axis=-1, keepdims=True)) + lam_init)
    q = q_ref[...]
    k = k_ref[...]
    vt_parts = ([vt_ref[:, :n_lo]] if n_lo else []) + [vt_ref[:, n_lo:]]
    q_lane = lax.broadcasted_iota(I32, q.shape, 1)
    k_lane = lax.broadcasted_iota(I32, k.shape, 1)
    outs = []
    for m in range(2):
        q_own = (q_lane < DH_C) if m == 0 else (q_lane >= DH_C)
        k_own = (k_lane < DH_C) if m == 0 else (k_lane >= DH_C)
        qm = jnp.where(q_own, q, jnp.broadcast_to(qa_ref[m], q.shape).astype(BF16))
        km = jnp.where(k_own, k, kpos_ref[m])
        parts = []
        if n_lo:
            parts.append(lax.dot_general(km[:n_lo], qm, _NT, preferred_element_type=F32))
        lg = lax.dot_general(km[n_lo:], qm, _NT, preferred_element_type=F32)
        parts.append(lg + diag_ref[...])
        outs.append(_softmax_pv_t(parts, vt_parts))
    o = outs[0] - lam * outs[1]
    o = o * lax.rsqrt(jnp.mean(o * o, axis=0, keepdims=True) + LN_EPS) * g_ref[...]
    o_ref[...] = (o * (1.0 - lam_init)).T.astype(o_ref.dtype)


def _diff_tile(qk3, vt3, qt, tq, lam_init, kpos, qa, diag, lq1, lk1, lq2, lk2, g):
    b, s, _ = qk3.shape
    sk = (qt + 1) * tq
    w = 2 * DH_C
    consts = (lq1, lk1, lq2, lk2, g)
    return pl.pallas_call(
        functools.partial(_diff_kernel, lam_init=lam_init),
        grid=(b, H_C),
        in_specs=[pl.BlockSpec((None, tq, w), lambda i, j: (i, qt, j)),
                  pl.BlockSpec((None, sk, w), lambda i, j: (i, 0, H_C + j)),
                  pl.BlockSpec((None, w, sk), lambda i, j: (i, j, 0)),
                  pl.BlockSpec((2, sk, w), lambda i, j: (0, 0, 0)),
                  pl.BlockSpec((None, 2, 1, w), lambda i, j: (j, 0, 0, 0)),
                  pl.BlockSpec((None, tq, tq), lambda i, j: (j, 0, 0))]
        + [_full(c) for c in consts],
        out_specs=pl.BlockSpec((None, tq, w), lambda i, j: (i, 0, j)),
        out_shape=jax.ShapeDtypeStruct((b, tq, H_C * w), BF16),
        compiler_params=_cparams(("parallel", "parallel")),
        name=f"diff_q{qt}",
    )(qk3, qk3, vt3, kpos, qa, diag, *consts)


def _pack_halves(x):
    lo = lax.bitcast_convert_type(x[:, :HALF].astype(BF16).astype(F32), I32)
    hi = lax.bitcast_convert_type(x[:, HALF:].astype(BF16).astype(F32), I32)
    return lax.shift_right_logical(lo, 16) | (hi & HI_MASK)


def _unpack_lo(w):
    return lax.bitcast_convert_type(lax.shift_left(w, 16), F32)


def _unpack_hi(w):
    return lax.bitcast_convert_type(w & HI_MASK, F32)


def _layer_norm_rows(z, g, b):
    mu = jnp.mean(z, axis=-1, keepdims=True)
    zc = z - mu
    var = jnp.mean(zc * zc, axis=-1, keepdims=True)
    return zc * lax.rsqrt(var + LN_EPS) * g + b


def _split2(x):
    hi = x.astype(BF16)
    mid = (x - hi.astype(F32)).astype(BF16)
    return hi, mid


def _outln_kernel(*refs, n_in):
    xs = refs[:n_in]
    ws = refs[n_in:2 * n_in]
    (h_ref, g_ref, b_ref, rw_ref, rb_ref, tri_ref,
     h1_ref, hp_ref, te_ref, tg_ref, rk_ref, cnt_ref) = refs[2 * n_in:]
    acc = None
    for x_ref, w_ref in zip(xs, ws):
        d = jnp.dot(x_ref[...], w_ref[...], preferred_element_type=F32)
        acc = d if acc is None else acc + d
    h1 = _layer_norm_rows(DN_ALPHA * h_ref[...] + acc, g_ref[...], b_ref[...])
    h1_ref[...] = h1
    hp_ref[...] = _pack_halves(h1)

    h_hi, h_mid = _split2(h1)
    dotf = lambda a, b: jnp.dot(a, b, preferred_element_type=F32)
    logits = dotf(h_hi, rw_ref[0]) + dotf(h_mid, rw_ref[0]) + dotf(h_hi, rw_ref[1]) + rb_ref[...]
    lane = lax.broadcasted_iota(I32, logits.shape, 1)
    vals, idxs = [], []
    for _ in range(TOP_K):
        mx = jnp.max(logits, axis=-1, keepdims=True)
        ix = jnp.min(jnp.where(logits == mx, lane, LANES), axis=-1, keepdims=True)
        vals.append(mx)
        idxs.append(ix)
        logits = jnp.where(lane == ix, -jnp.inf, logits)
    es = [jnp.exp(v - vals[0]) for v in vals]
    den = es[0] + es[1] + es[2] + es[3]

    @pl.when(pl.program_id(0) == 0)
    def _():
        cnt_ref[...] = jnp.zeros_like(cnt_ref)

    member = jnp.zeros(lane.shape, F32)
    for k in range(TOP_K):
        member = jnp.where(lane == idxs[k], 1.0, member)
    before = jnp.dot(tri_ref[...], member.astype(BF16), preferred_element_type=F32) + cnt_ref[...]
    te = jnp.zeros(lane.shape, I32)
    tg = jnp.zeros(lane.shape, F32)
    rk = jnp.zeros(lane.shape, F32)
    for k in range(TOP_K):
        rank_k = jnp.sum(jnp.where(lane == idxs[k], before, 0.0), axis=-1, keepdims=True)
        te = jnp.where(lane == k, idxs[k], te)
        tg = jnp.where(lane == k, es[k] / den, tg)
        rk = jnp.where(lane == k, rank_k, rk)
    te_ref[...] = te
    tg_ref[...] = tg
    rk_ref[...] = rk.astype(I32)
    cnt_ref[...] += jnp.sum(member, axis=0, keepdims=True)


def _outln(xs, ws, h2, g, b, rw, rb):
    m = h2.shape[0]
    tm = min(PROJ_TM, m)
    n_in = len(xs)
    row = lambda a: pl.BlockSpec((tm, a.shape[1]), lambda i: (i, 0))
    ri = lax.broadcasted_iota(I32, (tm, tm), 0)
    ci = lax.broadcasted_iota(I32, (tm, tm), 1)
    tri = (ci < ri).astype(BF16)
    rw3 = jnp.stack(_split2(rw))
    return pl.pallas_call(
        functools.partial(_outln_kernel, n_in=n_in),
        grid=(m // tm,),
        in_specs=[row(x) for x in xs] + [_full(w) for w in ws]
        + [row(h2), _full(g), _full(b), _full(rw3), _full(rb), _full(tri)],
        out_specs=[pl.BlockSpec((tm, D_MODEL), lambda i: (i, 0)),
                   pl.BlockSpec((tm, HALF), lambda i: (i, 0)),
                   pl.BlockSpec((tm, LANES), lambda i: (i, 0)),
                   pl.BlockSpec((tm, LANES), lambda i: (i, 0)),
                   pl.BlockSpec((tm, LANES), lambda i: (i, 0)),
                   pl.BlockSpec((1, LANES), lambda i: (0, 0))],
        out_shape=[jax.ShapeDtypeStruct((m, D_MODEL), F32),
                   jax.ShapeDtypeStruct((m, HALF), I32),
                   jax.ShapeDtypeStruct((m, LANES), I32),
                   jax.ShapeDtypeStruct((m, LANES), F32),
                   jax.ShapeDtypeStruct((m, LANES), I32),
                   jax.ShapeDtypeStruct((1, LANES), F32)],
        compiler_params=_cparams(("arbitrary",)),
        name="outln_router",
    )(*xs, *ws, h2, g, b, rw3, rb, tri)


def _sc_mesh():
    return plsc.VectorSubcoreMesh(core_axis_name="core", subcore_axis_name="subcore")


def _pad_windows(idx2):
    return jnp.pad(idx2, ((0, 0), (0, LANES - SC_WIN)))


def _sc_scatter(x, pos):
    t, d = x.shape
    nw = t // SC_WIN
    idx = _pad_windows(pos.reshape(nw, SC_WIN, TOP_K).transpose(0, 2, 1).reshape(nw * TOP_K, SC_WIN))

    @pl.kernel(out_type=jax.ShapeDtypeStruct((t * TOP_K, d), x.dtype), mesh=_sc_mesh())
    def scatter_rows(x_hbm, i_hbm, o_hbm):
        def body(x_vmem, i_vmem):
            pltpu.sync_copy(x_vmem, o_hbm.at[i_vmem.at[0, pl.ds(0, SC_WIN)]])

        pltpu.emit_pipeline(
            body,
            grid=(nw * TOP_K,),
            in_specs=[pl.BlockSpec((SC_WIN, d), lambda i: (i // TOP_K, 0)),
                      pl.BlockSpec((1, LANES), lambda i: (i, 0))],
            out_specs=[],
            core_axis_name=("core", "subcore"),
            dimension_semantics=(pltpu.PARALLEL,),
        )(x_hbm, i_hbm)

    return scatter_rows(x, idx)


def _sc_gather(y, idx):
    n = idx.shape[0]
    d = y.shape[1]
    idx2 = _pad_windows(idx.reshape(n // SC_WIN, SC_WIN))

    @pl.kernel(out_type=jax.ShapeDtypeStruct((n, d), y.dtype), mesh=_sc_mesh())
    def gather_rows(y_hbm, i_hbm, o_hbm):
        def body(i_vmem, o_vmem):
            pltpu.sync_copy(y_hbm.at[i_vmem.at[0, pl.ds(0, SC_WIN)]], o_vmem)

        pltpu.emit_pipeline(
            body,
            grid=(n // SC_WIN,),
            in_specs=[pl.BlockSpec((1, LANES), lambda i: (i, 0))],
            out_specs=[pl.BlockSpec((SC_WIN, d), lambda i: (i, 0))],
            core_axis_name=("core", "subcore"),
            dimension_semantics=(pltpu.PARALLEL,),
        )(i_hbm, o_hbm)

    return gather_rows(y, idx2)


def _ffn_kernel(vb_ref, ve_ref, lo_ref, hi_ref, first_ref, efirst_ref, x_ref, wgu_ref, bgu_ref,
                wdn_ref, bdn_ref, y_ref, wgu_s, wdn_s):
    v = pl.program_id(0)
    lo = lo_ref[v]
    hi = hi_ref[v]

    @pl.when(efirst_ref[v] == 1)
    def _():
        for c in range(D_MODEL // FFN_FC):
            rs = slice(c * FFN_FC, (c + 1) * FFN_FC)
            wgu_s[rs, :] = wgu_ref[rs, :].astype(BF16)
            wdn_s[rs, :] = wdn_ref[rs, :].astype(BF16)

    @pl.when(hi > lo)
    def _():
        xw = x_ref[...]
        xl = _unpack_lo(xw).astype(BF16)
        xh = _unpack_hi(xw).astype(BF16)
        acc = jnp.zeros((x_ref.shape[0], D_MODEL), F32)
        for c in range(D_FF // FFN_FC):
            gs = slice(c * FFN_FC, (c + 1) * FFN_FC)
            us = slice(D_FF + c * FFN_FC, D_FF + (c + 1) * FFN_FC)
            gate = (jnp.dot(xl, wgu_s[:HALF, gs], preferred_element_type=F32)
                    + jnp.dot(xh, wgu_s[HALF:, gs], preferred_element_type=F32) + bgu_ref[:, gs])
            up = (jnp.dot(xl, wgu_s[:HALF, us], preferred_element_type=F32)
                  + jnp.dot(xh, wgu_s[HALF:, us], preferred_element_type=F32) + bgu_ref[:, us])
            gate = jnp.minimum(gate, SWIGLU_LIMIT)
            up = jnp.clip(up, -SWIGLU_LIMIT, SWIGLU_LIMIT)
            act = (up + 1.0) * (gate * jax.nn.sigmoid(gate * SWIGLU_ALPHA))
            acc = acc + jnp.dot(act.astype(BF16), wdn_s[gs, :], preferred_element_type=F32)
        new = _pack_halves(acc + bdn_ref[...])
        row = lax.broadcasted_iota(I32, new.shape, 0)
        keep = jnp.where(first_ref[v] == 1, jnp.zeros_like(new), y_ref[...])
        y_ref[...] = jnp.where(row >= lo, jnp.where(row < hi, new, keep), keep)


def _ffn(visits, xs, layer, wgu, bgu, wdn, bdn):
    n_rows, hw = xs.shape
    n_vis = visits[0].shape[0]
    blk = lambda v, vb, *_: (vb[v], 0)
    exp = lambda v, vb, ve, *_: (layer, ve[v], 0, 0)
    grid_spec = pltpu.PrefetchScalarGridSpec(
        num_scalar_prefetch=len(visits),
        grid=(n_vis,),
        in_specs=[pl.BlockSpec((FFN_TM, hw), blk),
                  pl.BlockSpec((None, None, D_MODEL, 2 * D_FF), exp),
                  pl.BlockSpec((None, None, 1, 2 * D_FF), exp),
                  pl.BlockSpec((None, None, D_FF, D_MODEL), exp),
                  pl.BlockSpec((None, None, 1, D_MODEL), exp)],
        out_specs=pl.BlockSpec((FFN_TM, hw), blk),
        scratch_shapes=[pltpu.VMEM((D_MODEL, 2 * D_FF), BF16), pltpu.VMEM((D_FF, D_MODEL), BF16)],
    )
    return pl.pallas_call(
        _ffn_kernel,
        grid_spec=grid_spec,
        out_shape=jax.ShapeDtypeStruct((n_rows, hw), I32),
        compiler_params=_cparams(("arbitrary",)),
        name="moe_ffn",
    )(*visits, xs, wgu, bgu, wdn, bdn)


def _combine_kernel(yg_ref, h_ref, gate_ref, g_ref, b_ref, o_ref):
    gate = gate_ref[...]
    h = h_ref[...]
    zl = DN_ALPHA * h[:, :HALF]
    zh = DN_ALPHA * h[:, HALF:]
    for k in range(TOP_K):
        w = yg_ref[k]
        gk = gate[:, k:k + 1]
        zl = zl + gk * _unpack_lo(w)
        zh = zh + gk * _unpack_hi(w)
    mu = (jnp.sum(zl, axis=-1, keepdims=True) + jnp.sum(zh, axis=-1, keepdims=True)) * (1.0 / D_MODEL)
    zl = zl - mu
    zh = zh - mu
    var = (jnp.sum(zl * zl, axis=-1, keepdims=True)
           + jnp.sum(zh * zh, axis=-1, keepdims=True)) * (1.0 / D_MODEL)
    r = lax.rsqrt(var + LN_EPS)
    o_ref[:, :HALF] = zl * r * g_ref[:, :HALF] + b_ref[:, :HALF]
    o_ref[:, HALF:] = zh * r * g_ref[:, HALF:] + b_ref[:, HALF:]


def _combine(yg, h1, gates, g, b):
    m, d = h1.shape
    tm = min(COMB_TM, m)
    return pl.pallas_call(
        _combine_kernel,
        grid=(m // tm,),
        in_specs=[pl.BlockSpec((TOP_K, tm, HALF), lambda i: (0, i, 0)),
                  pl.BlockSpec((tm, d), lambda i: (i, 0)),
                  pl.BlockSpec((tm, LANES), lambda i: (i, 0)),
                  _full(g), _full(b)],
        out_specs=pl.BlockSpec((tm, d), lambda i: (i, 0)),
        out_shape=jax.ShapeDtypeStruct((m, d), F32),
        compiler_params=_cparams(("parallel",)),
        name="moe_combine",
    )(yg, h1, gates, g, b)


def _routing_tables(top_e, rank, counts):
    n_tok = top_e.shape[0]
    n_blocks = n_tok * TOP_K // FFN_TM
    n_vis = n_blocks + N_EXPERTS - 1
    end = jnp.cumsum(counts)
    start = end - counts
    onehot = top_e[:, :, None] == jnp.arange(N_EXPERTS, dtype=I32)[None, None, :]
    pos = (rank + jnp.sum(jnp.where(onehot, start[None, None, :], 0), axis=-1)).astype(I32)

    first = start // FFN_TM
    last = jnp.maximum(end - 1, 0) // FFN_TM
    nvis = jnp.where(counts > 0, last - first + 1, 0)
    vend = jnp.cumsum(nvis)
    vstart = vend - nvis
    total = vend[-1]
    v = jnp.arange(n_vis, dtype=I32)
    valid = v < total
    ev = jnp.sum((vend[None, :] <= jnp.minimum(v, total - 1)[:, None]).astype(I32), axis=1)
    ev = jnp.minimum(ev, N_EXPERTS - 1)
    pick = lambda tbl: jnp.sum(jnp.where(ev[:, None] == jnp.arange(N_EXPERTS, dtype=I32)[None, :],
                                         tbl[None, :], 0), axis=1)
    bv = jnp.where(valid, pick(first) + v - pick(vstart), n_blocks - 1).astype(I32)
    lo = jnp.where(valid, jnp.maximum(pick(start), bv * FFN_TM) - bv * FFN_TM, 0).astype(I32)
    hi = jnp.where(valid, jnp.minimum(pick(end), (bv + 1) * FFN_TM) - bv * FFN_TM, 0).astype(I32)
    prev_b = jnp.concatenate([jnp.full((1,), -1, I32), bv[:-1]])
    prev_e = jnp.concatenate([jnp.full((1,), -1, I32), ev[:-1]])
    fi = jnp.where(valid & (bv != prev_b), 1, 0).astype(I32)
    efi = jnp.where(ev != prev_e, 1, 0).astype(I32)
    return pos, (bv, ev.astype(I32), lo, hi, fi, efi)


def _moe(routed, layer, wgu, bgu, wdn, bdn, g, b):
    h1, h1p, te, gates, rk, cnt = routed
    n_tok = h1.shape[0]
    pos, visits = _routing_tables(te[:, :TOP_K], rk[:, :TOP_K], cnt[0, :N_EXPERTS].astype(I32))
    xs = _sc_scatter(h1p, pos)
    y = _ffn(visits, xs, layer, wgu, bgu, wdn, bdn)
    yg = _sc_gather(y, jnp.transpose(pos).reshape(-1)).reshape(TOP_K, n_tok, HALF)
    return _combine(yg, h1, gates, g, b)


def _row(a):
    return a.reshape(1, -1).astype(F32)


def _even_layer(h2, bsz, seq, w_in, q_norm, kv_norm, w_uq, w_uk, w_uv, w_iq, idx_g, idx_b, v_g, v_b,
                w_s, b_s, w_out, ln_g, ln_b, rw, rb):
    o4 = D_CQ + D_C + D_IDX + H_IDX
    w_in_p = jnp.concatenate(
        [w_in[:, :o4], jnp.zeros((D_MODEL, E_IN_PAD - E_IN_EVEN), F32), w_in[:, o4:]], axis=1)
    pad_idx = lambda a: jnp.pad(_row(a), ((0, 0), (0, LANES - D_IDX)))
    cq, ckv, kidx, widx, uv = _proj_even(h2, w_in_p.astype(BF16), _row(q_norm), _row(kv_norm),
                                         pad_idx(idx_g), pad_idx(idx_b))
    as3 = lambda a: a.reshape(bsz, seq, a.shape[-1])

    n_sel = min(TOPK_MAX, seq // 4)
    tq = min(DSA_TQ, seq)
    wuq = jnp.transpose(w_uq, (1, 0, 2)).astype(BF16)
    wiq = jnp.pad(jnp.transpose(w_iq, (1, 0, 2)),
                  ((0, 0), (0, 0), (0, LANES - D_IDX))).astype(BF16)
    wuk = w_uk.astype(BF16)
    eye = jnp.eye(H_A, dtype=F32)
    wuv = (w_uv[:, :, None, :] * eye[:, None, :, None]).reshape(H_A, D_C, H_A * DH_A).astype(BF16)
    o_a = jnp.concatenate(
        [_dsa_tile(as3(cq), as3(ckv), as3(kidx), as3(widx), qt, tq, n_sel, wuq, wuk, wiq, wuv)
         for qt in range(seq // tq)], axis=1)

    gbias = jnp.repeat(jnp.transpose(b_s), CG_B, axis=1)
    o_b = _gmlp(as3(uv), _row(v_g), _row(v_b), w_s, gbias)

    n_a = H_A * DH_A
    w_out_b = w_out.astype(BF16)
    return _outln([o_a.reshape(bsz * seq, n_a), o_b.reshape(bsz * seq, GMLP_WIDTH)],
                  [w_out_b[:n_a], w_out_b[n_a:]], h2, _row(ln_g), _row(ln_b), rw, rb)


def _odd_layer(h2, bsz, seq, layer, w_in, lq1, lk1, lq2, lk2, subln_g, w_out, ln_g, ln_b, rw, rb):
    lam_init = 0.8 - 0.6 * math.exp(-0.3 * layer)
    qk, vt3 = _proj_odd(h2, w_in.astype(BF16), seq)
    qk3 = qk.reshape(bsz, seq, 2 * QK_W_C)
    tq = min(DIFF_TQ, seq)
    kpos = jnp.stack([_alibi_key_cols(seq, DH_C), _alibi_key_cols(seq, 0)])
    qa0 = _alibi_query_cols(H_C)
    qa = jnp.stack([jnp.roll(qa0, DH_C, axis=-1), qa0], axis=1)
    diag = jnp.transpose(_alibi_diag(H_C, tq), (0, 2, 1))
    g_col = subln_g.reshape(-1, 1).astype(F32)
    o = jnp.concatenate(
        [_diff_tile(qk3, vt3, qt, tq, lam_init, kpos, qa, diag, _row(lq1), _row(lk1), _row(lq2),
                    _row(lk2), g_col) for qt in range(seq // tq)], axis=1)
    return _outln([o.reshape(bsz * seq, QK_W_C)], [w_out.astype(BF16)], h2, _row(ln_g), _row(ln_b),
                  rw, rb)


def kernel(x, ev_w_in, ev_q_norm, ev_kv_norm, ev_w_uq, ev_w_uk, ev_w_uv, ev_w_iq, ev_idx_k_g, ev_idx_k_b, ev_v_norm_g, ev_v_norm_b, ev_w_s, ev_b_s, ev_w_out, od_w_in, od_lambda_q1, od_lambda_k1, od_lambda_q2, od_lambda_k2, od_subln_g, od_w_out, ln1_g, ln1_b, ln2_g, ln2_b, router_w, router_b, exp_w_gu, exp_b_gu, exp_w_dn, exp_b_dn):
    bsz, seq, d = x.shape
    h2 = x.reshape(bsz * seq, d)
    for l in range(DEPTH):
        j = l // 2
        rw = jnp.pad(router_w[l], ((0, 0), (0, LANES - N_EXPERTS)))
        rb = jnp.pad(router_b[l], (0, LANES - N_EXPERTS), constant_values=-jnp.inf).reshape(1, LANES)
        if l % 2 == 0:
            routed = _even_layer(h2, bsz, seq, ev_w_in[j], ev_q_norm[j], ev_kv_norm[j],
                                 ev_w_uq[j], ev_w_uk[j], ev_w_uv[j], ev_w_iq[j], ev_idx_k_g[j],
                                 ev_idx_k_b[j], ev_v_norm_g[j], ev_v_norm_b[j], ev_w_s[j],
                                 ev_b_s[j], ev_w_out[j], ln1_g[l], ln1_b[l], rw, rb)
        else:
            routed = _odd_layer(h2, bsz, seq, l, od_w_in[j], od_lambda_q1[j], od_lambda_k1[j],
                                od_lambda_q2[j], od_lambda_k2[j], od_subln_g[j], od_w_out[j],
                                ln1_g[l], ln1_b[l], rw, rb)
        h2 = _moe(routed, l, exp_w_gu, exp_b_gu[:, :, None, :], exp_w_dn, exp_b_dn[:, :, None, :],
                  _row(ln2_g[l]), _row(ln2_b[l]))
    return h2.reshape(bsz, seq, d)
```

```python
import functools
import math

import jax
import jax.numpy as jnp
from jax import lax
from jax.experimental import pallas as pl
from jax.experimental.pallas import tpu as pltpu
from jax.experimental.pallas import tpu_sc as plsc

F32 = jnp.float32
BF16 = jnp.bfloat16
I32 = jnp.int32

D_MODEL = 1024
DEPTH = 4
CHUNK = 64
CHUNK_SHIFT = 6
H_A = 8
DH_A = 64
D_CQ = 256
D_C = 128
H_IDX = 4
D_IDX = 64
TOPK_MAX = 256
GMLP_CHUNK = 128
G_B = 8
GMLP_WIDTH = 512
CG_B = GMLP_WIDTH // G_B
H_C = 8
DH_C = 64
QK_W_C = H_C * 2 * DH_C
N_EXPERTS = 32
TOP_K = 4
D_FF = 1024
SWIGLU_ALPHA = 1.702
SWIGLU_LIMIT = 7.0
DN_ALPHA = (2 * DEPTH) ** 0.25
E_IN_EVEN = D_CQ + D_C + D_IDX + H_IDX + 2 * GMLP_WIDTH
LN_EPS = 1e-5
NEG = -1e30
LOG2E = 1.4426950408889634

LANES = 128
E_IN_PAD = 1536
INT_MIN = -(2 ** 31)
SIGN_FLIP = 0x7FFFFFFF
KEY_BITS = 32
KEY_BLOCK = 256
HALF = D_MODEL // 2
HI_MASK = -65536
SC_WIN = 64
POS_RADIX = 256
N_POS_COLS = 6

PROJ_TM = 512
DSA_TQ = 512
DIFF_TQ = 512
GMLP_TG = 512
FFN_TM = 512
FFN_FC = 512
COMB_TM = 512
VMEM_LIMIT = 56 * 1024 * 1024


def _cparams(sem):
    return pltpu.CompilerParams(dimension_semantics=sem, vmem_limit_bytes=VMEM_LIMIT)


def _full(a):
    zeros = (0,) * a.ndim
    return pl.BlockSpec(a.shape, lambda *_: zeros)


def _alibi_slopes_l2(n):
    return jnp.exp2(-8.0 * jnp.arange(1, n + 1, dtype=F32) / n) * LOG2E


def _alibi_query_cols(n):
    s = _alibi_slopes_l2(n)
    hi = s.astype(BF16).astype(F32)
    mid = (s - hi).astype(BF16).astype(F32)
    lo = (s - hi - mid).astype(BF16).astype(F32)
    cols = jnp.stack([hi * POS_RADIX, hi, mid * POS_RADIX, mid, lo * POS_RADIX, lo], axis=-1)
    return jnp.pad(cols, ((0, 0), (0, LANES - N_POS_COLS))).reshape(n, 1, LANES)


def _alibi_key_cols(seq, lane0):
    s = jnp.arange(seq, dtype=I32)
    a = (s // POS_RADIX).astype(F32)
    b = (s % POS_RADIX).astype(F32)
    cols = jnp.stack([a, b, a, b, a, b], axis=-1)
    return jnp.pad(cols, ((0, 0), (lane0, LANES - N_POS_COLS - lane0))).astype(BF16)


def _alibi_diag(n, tq):
    t = jnp.arange(tq, dtype=I32)[:, None]
    s = jnp.arange(tq, dtype=I32)[None, :]
    ok = (s >> CHUNK_SHIFT) <= (t >> CHUNK_SHIFT)
    ahead = jnp.maximum(s - t, 0).astype(F32)
    corr = -2.0 * _alibi_slopes_l2(n)[:, None, None] * ahead[None]
    return jnp.where(ok[None], corr, NEG)


def _proj_even_kernel(x_ref, w_ref, qg_ref, kvg_ref, ig_ref, ib_ref,
                      cq_ref, ckv_ref, kidx_ref, widx_ref, uv_ref):
    acc = jnp.dot(x_ref[...].astype(BF16), w_ref[...], preferred_element_type=F32)
    cq = acc[:, :D_CQ]
    cq_ref[...] = (cq * lax.rsqrt(jnp.mean(cq * cq, axis=-1, keepdims=True) + LN_EPS)
                   * qg_ref[...]).astype(cq_ref.dtype)
    kv = acc[:, D_CQ:D_CQ + D_C]
    ckv_ref[...] = (kv * lax.rsqrt(jnp.mean(kv * kv, axis=-1, keepdims=True) + LN_EPS)
                    * kvg_ref[...]).astype(ckv_ref.dtype)
    blk = acc[:, D_CQ + D_C:D_CQ + D_C + LANES]
    lane = lax.broadcasted_iota(I32, blk.shape, 1)
    is_k = lane < D_IDX
    mu = jnp.sum(jnp.where(is_k, blk, 0.0), axis=-1, keepdims=True) * (1.0 / D_IDX)
    kc = jnp.where(is_k, blk - mu, 0.0)
    var = jnp.sum(kc * kc, axis=-1, keepdims=True) * (1.0 / D_IDX)
    kidx_ref[...] = jnp.where(is_k, kc * lax.rsqrt(var + LN_EPS) * ig_ref[...] + ib_ref[...],
                              0.0).astype(kidx_ref.dtype)
    widx_ref[...] = jnp.where(is_k, 0.0, blk * (H_IDX ** -0.5))
    uv_ref[...] = acc[:, E_IN_PAD - 2 * GMLP_WIDTH:]


def _proj_even(x2, w, qg, kvg, ig, ib):
    m, k = x2.shape
    tm = min(PROJ_TM, m)
    row = lambda n: pl.BlockSpec((tm, n), lambda i: (i, 0))
    consts = (qg, kvg, ig, ib)
    return pl.pallas_call(
        _proj_even_kernel,
        grid=(m // tm,),
        in_specs=[row(k), _full(w)] + [_full(c) for c in consts],
        out_specs=[row(D_CQ), row(D_C), row(LANES), row(LANES), row(2 * GMLP_WIDTH)],
        out_shape=[jax.ShapeDtypeStruct((m, D_CQ), BF16),
                   jax.ShapeDtypeStruct((m, D_C), BF16),
                   jax.ShapeDtypeStruct((m, LANES), BF16),
                   jax.ShapeDtypeStruct((m, LANES), F32),
                   jax.ShapeDtypeStruct((m, 2 * GMLP_WIDTH), F32)],
        compiler_params=_cparams(("parallel",)),
        name="proj_even",
    )(x2, w, *consts)


def _proj_odd_kernel(x_ref, w_ref, qk_ref, vt_ref):
    acc = jnp.dot(x_ref[...].astype(BF16), w_ref[...], preferred_element_type=F32)
    qk_ref[:, :QK_W_C] = (acc[:, :QK_W_C] * (DH_C ** -0.5 * LOG2E)).astype(qk_ref.dtype)
    qk_ref[:, QK_W_C:] = acc[:, QK_W_C:2 * QK_W_C].astype(qk_ref.dtype)
    w = 2 * DH_C
    for h in range(H_C):
        v = acc[:, 2 * QK_W_C + h * w:2 * QK_W_C + (h + 1) * w]
        vt_ref[h * w:(h + 1) * w, :] = v.T.astype(vt_ref.dtype)


def _proj_odd(x2, w, seq):
    m, k = x2.shape
    tm = min(PROJ_TM, seq)
    per_seq = seq // tm
    return pl.pallas_call(
        _proj_odd_kernel,
        grid=(m // tm,),
        in_specs=[pl.BlockSpec((tm, k), lambda i: (i, 0)), _full(w)],
        out_specs=[pl.BlockSpec((tm, 2 * QK_W_C), lambda i: (i, 0)),
                   pl.BlockSpec((None, QK_W_C, tm), lambda i: (i // per_seq, 0, i % per_seq))],
        out_shape=[jax.ShapeDtypeStruct((m, 2 * QK_W_C), BF16),
                   jax.ShapeDtypeStruct((m // seq, QK_W_C, seq), BF16)],
        compiler_params=_cparams(("parallel",)),
        name="proj_odd",
    )(x2, w)


_NT = (((1,), (1,)), ((), ()))


COL_GROUPS = 8


def _col_reduce(x, red, comb):
    step = x.shape[0] // COL_GROUPS
    vals = [red(x[i * step:(i + 1) * step], axis=0, keepdims=True) for i in range(COL_GROUPS)]
    while len(vals) > 1:
        vals = [comb(vals[i], vals[i + 1]) for i in range(0, len(vals), 2)]
    return vals[0]


def _softmax_pv_t(parts, vt_parts):
    m = None
    for lg in parts:
        mx = _col_reduce(lg, jnp.max, jnp.maximum)
        m = mx if m is None else jnp.maximum(m, mx)
    l = None
    o = None
    for lg, vt in zip(parts, vt_parts):
        p = jnp.exp2(lg - m)
        s = _col_reduce(p, jnp.sum, jnp.add)
        d = jnp.dot(vt, p.astype(BF16), preferred_element_type=F32)
        l = s if l is None else l + s
        o = d if o is None else o + d
    return o * (1.0 / l)


def _softmax_pv(parts, kv_parts):
    m = None
    for lg in parts:
        mx = jnp.max(lg, axis=-1, keepdims=True)
        m = mx if m is None else jnp.maximum(m, mx)
    l = None
    o = None
    for lg, kv in zip(parts, kv_parts):
        p = jnp.exp2(lg - m)
        s = jnp.sum(p, axis=-1, keepdims=True)
        d = jnp.dot(p.astype(BF16), kv, preferred_element_type=F32)
        l = s if l is None else l + s
        o = d if o is None else o + d
    return o * (1.0 / l)


def _dsa_kernel(cq_ref, ckv_ref, kidx_ref, widx_ref, wuq_ref, wuk_ref, wiq_ref, wuv_ref, srow_ref,
                diag_ref, o_ref, bias_ref, key_ref, *, q0, n_sel):
    tq = cq_ref.shape[0]
    sk = ckv_ref.shape[0]
    cqb = cq_ref[...]
    widx = widx_ref[...]

    qis = [jnp.dot(cqb, wiq_ref[h], preferred_element_type=F32).astype(BF16) for h in range(H_IDX)]
    for c in range(sk // KEY_BLOCK):
        sl = slice(c * KEY_BLOCK, (c + 1) * KEY_BLOCK)
        score = None
        for h in range(H_IDX):
            r = lax.dot_general(qis[h], kidx_ref[sl, :], _NT, preferred_element_type=F32)
            term = jnp.maximum(r * (D_IDX ** -0.5), 0.0) * widx[:, D_IDX + h:D_IDX + h + 1]
            score = term if score is None else score + term
        if (c + 1) * KEY_BLOCK > sk - tq:
            t_pos = q0 + lax.broadcasted_iota(I32, (tq, KEY_BLOCK), 0)
            s_pos = c * KEY_BLOCK + lax.broadcasted_iota(I32, (tq, KEY_BLOCK), 1)
            score = jnp.where((s_pos >> CHUNK_SHIFT) <= (t_pos >> CHUNK_SHIFT), score, NEG)
        bits = lax.bitcast_convert_type(score, I32)
        kb = jnp.where(bits < 0, bits ^ SIGN_FLIP, bits)
        key_ref[:, sl] = jnp.where(kb == -1, 0, kb)

    def count(mask):
        return jnp.sum(jnp.where(mask, 1.0, 0.0), axis=-1, keepdims=True)

    nsel = float(n_sel)
    cur = jnp.where(count(key_ref[...] >= 0) >= nsel, 0, INT_MIN).astype(I32)

    def search(i, cur):
        cand = cur | lax.shift_left(jnp.int32(1), KEY_BITS - 2 - i)
        return jnp.where(count(key_ref[...] >= cand) >= nsel, cand, cur)

    cur = lax.fori_loop(0, KEY_BITS - 1, search, cur)

    need = nsel - count(key_ref[...] > cur)
    ri = lax.broadcasted_iota(I32, (LANES, LANES), 0)
    ci = lax.broadcasted_iota(I32, (LANES, LANES), 1)
    tri = jnp.where(ri < ci, 1.0, 0.0).astype(BF16)
    ones_b = jnp.ones((LANES, LANES), BF16)
    off = jnp.zeros((tq, 1), F32)
    for c in range(sk // LANES):
        sl = slice(c * LANES, (c + 1) * LANES)
        keyc = key_ref[:, sl]
        eqc = jnp.where(keyc == cur, 1.0, 0.0)
        eqb = eqc.astype(BF16)
        rank = jnp.dot(eqb, tri, preferred_element_type=F32) + off
        sel = jnp.where(keyc > cur, 1.0, jnp.where(rank < need, eqc, 0.0))
        bias_ref[:, sl] = jnp.where(sel > 0.5, 0.0, NEG)
        off = off + jnp.dot(eqb, ones_b, preferred_element_type=F32)[:, :1]

    n_lo = sk - tq
    kv_parts = ([ckv_ref[:n_lo, :]] if n_lo else []) + [ckv_ref[n_lo:, :]]
    acc = jnp.zeros((tq, H_A * DH_A), F32)
    for h in range(H_A):
        qh = jnp.dot(cqb, wuq_ref[h], preferred_element_type=F32).astype(BF16)
        ql = (jnp.dot(qh, wuk_ref[h], preferred_element_type=F32)
              * (DH_A ** -0.5 * LOG2E)).astype(BF16)
        parts = []
        if n_lo:
            lg = lax.dot_general(ql, kv_parts[0], _NT, preferred_element_type=F32)
            parts.append(lg + bias_ref[:, :n_lo] + srow_ref[h][:, :n_lo])
        lg = lax.dot_general(ql, kv_parts[-1], _NT, preferred_element_type=F32)
        parts.append(lg + bias_ref[:, n_lo:] + diag_ref[h])
        ol = _softmax_pv(parts, kv_parts)
        acc = acc + jnp.dot(ol.astype(BF16), wuv_ref[h], preferred_element_type=F32)
    o_ref[...] = acc.astype(o_ref.dtype)


def _dsa_tile(cq3, ckv3, kidx3, widx3, qt, tq, n_sel, wuq, wuk, wiq, wuv):
    b, s, _ = cq3.shape
    sk = (qt + 1) * tq
    slopes = _alibi_slopes_l2(H_A)
    srow = (slopes[:, None] * jnp.arange(sk, dtype=F32)[None, :]).reshape(H_A, 1, sk)
    diag = _alibi_diag(H_A, tq) + srow[:, :, sk - tq:]
    consts = (wuq, wuk, wiq, wuv, srow, diag)
    return pl.pallas_call(
        functools.partial(_dsa_kernel, q0=qt * tq, n_sel=n_sel),
        grid=(b,),
        in_specs=[pl.BlockSpec((None, tq, D_CQ), lambda i: (i, qt, 0)),
                  pl.BlockSpec((None, sk, D_C), lambda i: (i, 0, 0)),
                  pl.BlockSpec((None, sk, LANES), lambda i: (i, 0, 0)),
                  pl.BlockSpec((None, tq, LANES), lambda i: (i, qt, 0))]
        + [_full(c) for c in consts],
        out_specs=pl.BlockSpec((None, tq, H_A * DH_A), lambda i: (i, 0, 0)),
        out_shape=jax.ShapeDtypeStruct((b, tq, H_A * DH_A), BF16),
        scratch_shapes=[pltpu.VMEM((tq, sk), F32), pltpu.VMEM((tq, sk), I32)],
        compiler_params=_cparams(("parallel",)),
        name=f"dsa_q{qt}",
    )(cq3, ckv3, kidx3, widx3, *consts)


def _gelu(x):
    return 0.5 * x * (1.0 + jnp.tanh(0.7978845608028654 * (x + 0.044715 * (x * x * x))))


def _gmlp_kernel(u_ref, v_ref, vg_ref, vb_ref, ws_ref, bias_ref, o_ref):
    tg = u_ref.shape[0]
    ri = lax.broadcasted_iota(I32, (GMLP_CHUNK, GMLP_CHUNK), 0)
    ci = lax.broadcasted_iota(I32, (GMLP_CHUNK, GMLP_CHUNK), 1)
    tril = ri >= ci
    grp = lax.broadcasted_iota(I32, (GMLP_CHUNK, GMLP_WIDTH), 1) // CG_B
    ws = [jnp.where(tril, ws_ref[g], 0.0).astype(BF16) for g in range(G_B)]
    for c in range(tg // GMLP_CHUNK):
        sl = slice(c * GMLP_CHUNK, (c + 1) * GMLP_CHUNK)
        u = _gelu(u_ref[sl, :])
        v = _gelu(v_ref[sl, :])
        mu = jnp.mean(v, axis=-1, keepdims=True)
        vc = v - mu
        var = jnp.mean(vc * vc, axis=-1, keepdims=True)
        vn = (vc * lax.rsqrt(var + LN_EPS) * vg_ref[...] + vb_ref[...]).astype(BF16)
        mixed = bias_ref[...]
        for g in range(G_B):
            r = jnp.dot(ws[g], vn, preferred_element_type=F32)
            mixed = mixed + jnp.where(grp == g, r, 0.0)
        o_ref[sl, :] = (u * mixed).astype(o_ref.dtype)


def _gmlp(uv3, vg, vb, ws, bias):
    b, s, _ = uv3.shape
    tg = min(GMLP_TG, s)
    consts = (vg, vb, ws, bias)
    return pl.pallas_call(
        _gmlp_kernel,
        grid=(b, s // tg),
        in_specs=[pl.BlockSpec((None, tg, GMLP_WIDTH), lambda i, j: (i, j, 0)),
                  pl.BlockSpec((None, tg, GMLP_WIDTH), lambda i, j: (i, j, 1))]
        + [_full(c) for c in consts],
        out_specs=pl.BlockSpec((None, tg, GMLP_WIDTH), lambda i, j: (i, j, 0)),
        out_shape=jax.ShapeDtypeStruct((b, s, GMLP_WIDTH), BF16),
        compiler_params=_cparams(("parallel", "parallel")),
        name="gmlp",
    )(uv3, uv3, *consts)


def _diff_kernel(q_ref, k_ref, vt_ref, kpos_ref, qa_ref, diag_ref, lq1_ref, lk1_ref, lq2_ref, lk2_ref,
                 g_ref, o_ref, *, lam_init):
    tq = q_ref.shape[0]
    sk = k_ref.shape[0]
    n_lo = sk - tq
    lam = (jnp.exp(jnp.sum(lq1_ref[...] * lk1_ref[...], axis=-1, keepdims=True))
           - jnp.exp(jnp.sum(lq2_ref[...] * lk2_ref[...], axis=-1, keepdims=True)) + lam_init)
    q = q_ref[...]
    k = k_ref[...]
    vt_parts = ([vt_ref[:, :n_lo]] if n_lo else []) + [vt_ref[:, n_lo:]]
    q_lane = lax.broadcasted_iota(I32, q.shape, 1)
    k_lane = lax.broadcasted_iota(I32, k.shape, 1)
    outs = []
    for m in range(2):
        q_own = (q_lane < DH_C) if m == 0 else (q_lane >= DH_C)
        k_own = (k_lane < DH_C) if m == 0 else (k_lane >= DH_C)
        qm = jnp.where(q_own, q, jnp.broadcast_to(qa_ref[m], q.shape).astype(BF16))
        km = jnp.where(k_own, k, kpos_ref[m])
        parts = []
        if n_lo:
            parts.append(lax.dot_general(km[:n_lo], qm, _NT, preferred_element_type=F32))
        lg = lax.dot_general(km[n_lo:], qm, _NT, preferred_element_type=F32)
        parts.append(lg + diag_ref[...])
        outs.append(_softmax_pv_t(parts, vt_parts))
    o = outs[0] - lam * outs[1]
    o = o * lax.rsqrt(jnp.mean(o * o, axis=0, keepdims=True) + LN_EPS) * g_ref[...]
    o_ref[...] = (o * (1.0 - lam_init)).T.astype(o_ref.dtype)


def _diff_tile(qk3, vt3, qt, tq, lam_init, kpos, qa, diag, lq1, lk1, lq2, lk2, g):
    b, s, _ = qk3.shape
    sk = (qt + 1) * tq
    w = 2 * DH_C
    consts = (lq1, lk1, lq2, lk2, g)
    return pl.pallas_call(
        functools.partial(_diff_kernel, lam_init=lam_init),
        grid=(b, H_C),
        in_specs=[pl.BlockSpec((None, tq, w), lambda i, j: (i, qt, j)),
                  pl.BlockSpec((None, sk, w), lambda i, j: (i, 0, H_C + j)),
                  pl.BlockSpec((None, w, sk), lambda i, j: (i, j, 0)),
                  pl.BlockSpec((2, sk, w), lambda i, j: (0, 0, 0)),
                  pl.BlockSpec((None, 2, 1, w), lambda i, j: (j, 0, 0, 0)),
                  pl.BlockSpec((None, tq, tq), lambda i, j: (j, 0, 0))]
        + [_full(c) for c in consts],
        out_specs=pl.BlockSpec((None, tq, w), lambda i, j: (i, 0, j)),
        out_shape=jax.ShapeDtypeStruct((b, tq, H_C * w), BF16),
        compiler_params=_cparams(("parallel", "parallel")),
        name=f"diff_q{qt}",
    )(qk3, qk3, vt3, kpos, qa, diag, *consts)


def _pack_halves(x):
    lo = lax.bitcast_convert_type(x[:, :HALF].astype(BF16).astype(F32), I32)
    hi = lax.bitcast_convert_type(x[:, HALF:].astype(BF16).astype(F32), I32)
    return lax.shift_right_logical(lo, 16) | (hi & HI_MASK)


def _unpack_lo(w):
    return lax.bitcast_convert_type(lax.shift_left(w, 16), F32)


def _unpack_hi(w):
    return lax.bitcast_convert_type(w & HI_MASK, F32)


def _layer_norm_rows(z, g, b):
    mu = jnp.mean(z, axis=-1, keepdims=True)
    zc = z - mu
    var = jnp.mean(zc * zc, axis=-1, keepdims=True)
    return zc * lax.rsqrt(var + LN_EPS) * g + b


def _split2(x):
    hi = x.astype(BF16)
    mid = (x - hi.astype(F32)).astype(BF16)
    return hi, mid


def _outln_kernel(*refs, pieces):
    n_x = sum(pieces)
    n_in = len(pieces)
    ws = refs[n_x:n_x + n_in]
    (h_ref, g_ref, b_ref, rw_ref, rb_ref, tri_ref,
     h1_ref, hp_ref, te_ref, tg_ref, rk_ref, cnt_ref) = refs[n_x + n_in:]
    acc = None
    first = 0
    for n_p, w_ref in zip(pieces, ws):
        x = refs[first][...]
        for p in range(1, n_p):
            x = jnp.where(pl.program_id(0) % n_p == p, refs[first + p][...], x)
        first += n_p
        d = jnp.dot(x, w_ref[...], preferred_element_type=F32)
        acc = d if acc is None else acc + d
    h1 = _layer_norm_rows(DN_ALPHA * h_ref[...] + acc, g_ref[...], b_ref[...])
    h1_ref[...] = h1
    hp_ref[...] = _pack_halves(h1)

    h_hi, h_mid = _split2(h1)
    dotf = lambda a, b: jnp.dot(a, b, preferred_element_type=F32)
    logits = dotf(h_hi, rw_ref[0]) + dotf(h_mid, rw_ref[0]) + dotf(h_hi, rw_ref[1]) + rb_ref[...]
    lane = lax.broadcasted_iota(I32, logits.shape, 1)
    vals, idxs = [], []
    for _ in range(TOP_K):
        mx = jnp.max(logits, axis=-1, keepdims=True)
        ix = jnp.min(jnp.where(logits == mx, lane, LANES), axis=-1, keepdims=True)
        vals.append(mx)
        idxs.append(ix)
        logits = jnp.where(lane == ix, -jnp.inf, logits)
    es = [jnp.exp(v - vals[0]) for v in vals]
    den = es[0] + es[1] + es[2] + es[3]

    @pl.when(pl.program_id(0) == 0)
    def _():
        cnt_ref[...] = jnp.zeros_like(cnt_ref)

    member = jnp.zeros(lane.shape, F32)
    for k in range(TOP_K):
        member = jnp.where(lane == idxs[k], 1.0, member)
    before = jnp.dot(tri_ref[...], member.astype(BF16), preferred_element_type=F32) + cnt_ref[...]
    te = jnp.zeros(lane.shape, I32)
    tg = jnp.zeros(lane.shape, F32)
    rk = jnp.zeros(lane.shape, F32)
    for k in range(TOP_K):
        rank_k = jnp.sum(jnp.where(lane == idxs[k], before, 0.0), axis=-1, keepdims=True)
        te = jnp.where(lane == k, idxs[k], te)
        tg = jnp.where(lane == k, es[k] / den, tg)
        rk = jnp.where(lane == k, rank_k, rk)
    te_ref[...] = te
    tg_ref[...] = tg
    rk_ref[...] = rk.astype(I32)
    cnt_ref[...] += jnp.sum(member, axis=0, keepdims=True)


def _outln(xs, ws, h2, g, b, rw, rb):
    m = h2.shape[0]
    tm = min(PROJ_TM, m)
    row = lambda a: pl.BlockSpec((tm, a.shape[1]), lambda i: (i, 0))
    x_specs = []
    for op in xs:
        n_p = len(op)
        for q, a in enumerate(op):
            assert a.shape[0] * n_p == m
            x_specs.append(pl.BlockSpec((tm, a.shape[1]),
                                        lambda i, q=q, n_p=n_p: (jnp.maximum(i - q, 0) // n_p, 0)))
    ri = lax.broadcasted_iota(I32, (tm, tm), 0)
    ci = lax.broadcasted_iota(I32, (tm, tm), 1)
    tri = (ci < ri).astype(BF16)
    rw3 = jnp.stack(_split2(rw))
    flat = [a for op in xs for a in op]
    return pl.pallas_call(
        functools.partial(_outln_kernel, pieces=tuple(len(op) for op in xs)),
        grid=(m // tm,),
        in_specs=x_specs + [_full(w) for w in ws]
        + [row(h2), _full(g), _full(b), _full(rw3), _full(rb), _full(tri)],
        out_specs=[pl.BlockSpec((tm, D_MODEL), lambda i: (i, 0)),
                   pl.BlockSpec((tm, HALF), lambda i: (i, 0)),
                   pl.BlockSpec((tm, LANES), lambda i: (i, 0)),
                   pl.BlockSpec((tm, LANES), lambda i: (i, 0)),
                   pl.BlockSpec((tm, LANES), lambda i: (i, 0)),
                   pl.BlockSpec((1, LANES), lambda i: (0, 0))],
        out_shape=[jax.ShapeDtypeStruct((m, D_MODEL), F32),
                   jax.ShapeDtypeStruct((m, HALF), I32),
                   jax.ShapeDtypeStruct((m, LANES), I32),
                   jax.ShapeDtypeStruct((m, LANES), F32),
                   jax.ShapeDtypeStruct((m, LANES), I32),
                   jax.ShapeDtypeStruct((1, LANES), F32)],
        compiler_params=_cparams(("arbitrary",)),
        name="outln_router",
    )(*flat, *ws, h2, g, b, rw3, rb, tri)


def _sc_mesh():
    return plsc.VectorSubcoreMesh(core_axis_name="core", subcore_axis_name="subcore")


def _pad_windows(idx2):
    return jnp.pad(idx2, ((0, 0), (0, LANES - SC_WIN)))


def _sc_scatter(x, pos):
    t, d = x.shape
    nw = t // SC_WIN
    idx = _pad_windows(pos.reshape(nw, SC_WIN, TOP_K).transpose(0, 2, 1).reshape(nw * TOP_K, SC_WIN))

    @pl.kernel(out_type=jax.ShapeDtypeStruct((t * TOP_K, d), x.dtype), mesh=_sc_mesh())
    def scatter_rows(x_hbm, i_hbm, o_hbm):
        def body(x_vmem, i_vmem):
            pltpu.sync_copy(x_vmem, o_hbm.at[i_vmem.at[0, pl.ds(0, SC_WIN)]])

        pltpu.emit_pipeline(
            body,
            grid=(nw * TOP_K,),
            in_specs=[pl.BlockSpec((SC_WIN, d), lambda i: (i // TOP_K, 0)),
                      pl.BlockSpec((1, LANES), lambda i: (i, 0))],
            out_specs=[],
            core_axis_name=("core", "subcore"),
            dimension_semantics=(pltpu.PARALLEL,),
        )(x_hbm, i_hbm)

    return scatter_rows(x, idx)


def _sc_gather(y, idx):
    n = idx.shape[0]
    d = y.shape[1]
    idx2 = _pad_windows(idx.reshape(n // SC_WIN, SC_WIN))

    @pl.kernel(out_type=jax.ShapeDtypeStruct((n, d), y.dtype), mesh=_sc_mesh())
    def gather_rows(y_hbm, i_hbm, o_hbm):
        def body(i_vmem, o_vmem):
            pltpu.sync_copy(y_hbm.at[i_vmem.at[0, pl.ds(0, SC_WIN)]], o_vmem)

        pltpu.emit_pipeline(
            body,
            grid=(n // SC_WIN,),
            in_specs=[pl.BlockSpec((1, LANES), lambda i: (i, 0))],
            out_specs=[pl.BlockSpec((SC_WIN, d), lambda i: (i, 0))],
            core_axis_name=("core", "subcore"),
            dimension_semantics=(pltpu.PARALLEL,),
        )(i_hbm, o_hbm)

    return gather_rows(y, idx2)


def _ffn_kernel(vb_ref, ve_ref, lo_ref, hi_ref, first_ref, efirst_ref, x_ref, wgu_ref, bgu_ref,
                wdn_ref, bdn_ref, y_ref, wgu_s, wdn_s):
    v = pl.program_id(0)
    lo = lo_ref[v]
    hi = hi_ref[v]

    @pl.when(efirst_ref[v] == 1)
    def _():
        for c in range(D_MODEL // FFN_FC):
            rs = slice(c * FFN_FC, (c + 1) * FFN_FC)
            wgu_s[rs, :] = wgu_ref[rs, :].astype(BF16)
            wdn_s[rs, :] = wdn_ref[rs, :].astype(BF16)

    @pl.when(hi > lo)
    def _():
        xw = x_ref[...]
        xl = _unpack_lo(xw).astype(BF16)
        xh = _unpack_hi(xw).astype(BF16)
        acc = jnp.zeros((x_ref.shape[0], D_MODEL), F32)
        for c in range(D_FF // FFN_FC):
            gs = slice(c * FFN_FC, (c + 1) * FFN_FC)
            us = slice(D_FF + c * FFN_FC, D_FF + (c + 1) * FFN_FC)
            gate = (jnp.dot(xl, wgu_s[:HALF, gs], preferred_element_type=F32)
                    + jnp.dot(xh, wgu_s[HALF:, gs], preferred_element_type=F32) + bgu_ref[:, gs])
            up = (jnp.dot(xl, wgu_s[:HALF, us], preferred_element_type=F32)
                  + jnp.dot(xh, wgu_s[HALF:, us], preferred_element_type=F32) + bgu_ref[:, us])
            gate = jnp.minimum(gate, SWIGLU_LIMIT)
            up = jnp.clip(up, -SWIGLU_LIMIT, SWIGLU_LIMIT)
            act = (up + 1.0) * (gate * jax.nn.sigmoid(gate * SWIGLU_ALPHA))
            acc = acc + jnp.dot(act.astype(BF16), wdn_s[gs, :], preferred_element_type=F32)
        new = _pack_halves(acc + bdn_ref[...])
        row = lax.broadcasted_iota(I32, new.shape, 0)
        keep = jnp.where(first_ref[v] == 1, jnp.zeros_like(new), y_ref[...])
        y_ref[...] = jnp.where(row >= lo, jnp.where(row < hi, new, keep), keep)


def _ffn(visits, xs, layer, wgu, bgu, wdn, bdn):
    n_rows, hw = xs.shape
    n_vis = visits[0].shape[0]
    blk = lambda v, vb, *_: (vb[v], 0)
    exp = lambda v, vb, ve, *_: (layer, ve[v], 0, 0)
    grid_spec = pltpu.PrefetchScalarGridSpec(
        num_scalar_prefetch=len(visits),
        grid=(n_vis,),
        in_specs=[pl.BlockSpec((FFN_TM, hw), blk),
                  pl.BlockSpec((None, None, D_MODEL, 2 * D_FF), exp),
                  pl.BlockSpec((None, None, 1, 2 * D_FF), exp),
                  pl.BlockSpec((None, None, D_FF, D_MODEL), exp),
                  pl.BlockSpec((None, None, 1, D_MODEL), exp)],
        out_specs=pl.BlockSpec((FFN_TM, hw), blk),
        scratch_shapes=[pltpu.VMEM((D_MODEL, 2 * D_FF), BF16), pltpu.VMEM((D_FF, D_MODEL), BF16)],
    )
    return pl.pallas_call(
        _ffn_kernel,
        grid_spec=grid_spec,
        out_shape=jax.ShapeDtypeStruct((n_rows, hw), I32),
        compiler_params=_cparams(("arbitrary",)),
        name="moe_ffn",
    )(*visits, xs, wgu, bgu, wdn, bdn)


def _combine_kernel(yg_ref, h_ref, gate_ref, g_ref, b_ref, o_ref):
    gate = gate_ref[...]
    h = h_ref[...]
    zl = DN_ALPHA * h[:, :HALF]
    zh = DN_ALPHA * h[:, HALF:]
    for k in range(TOP_K):
        w = yg_ref[k]
        gk = gate[:, k:k + 1]
        zl = zl + gk * _unpack_lo(w)
        zh = zh + gk * _unpack_hi(w)
    mu = (jnp.sum(zl, axis=-1, keepdims=True) + jnp.sum(zh, axis=-1, keepdims=True)) * (1.0 / D_MODEL)
    zl = zl - mu
    zh = zh - mu
    var = (jnp.sum(zl * zl, axis=-1, keepdims=True)
           + jnp.sum(zh * zh, axis=-1, keepdims=True)) * (1.0 / D_MODEL)
    r = lax.rsqrt(var + LN_EPS)
    o_ref[:, :HALF] = zl * r * g_ref[:, :HALF] + b_ref[:, :HALF]
    o_ref[:, HALF:] = zh * r * g_ref[:, HALF:] + b_ref[:, HALF:]


def _combine(yg, h1, gates, g, b):
    m, d = h1.shape
    tm = min(COMB_TM, m)
    return pl.pallas_call(
        _combine_kernel,
        grid=(m // tm,),
        in_specs=[pl.BlockSpec((TOP_K, tm, HALF), lambda i: (0, i, 0)),
                  pl.BlockSpec((tm, d), lambda i: (i, 0)),
                  pl.BlockSpec((tm, LANES), lambda i: (i, 0)),
                  _full(g), _full(b)],
        out_specs=pl.BlockSpec((tm, d), lambda i: (i, 0)),
        out_shape=jax.ShapeDtypeStruct((m, d), F32),
        compiler_params=_cparams(("parallel",)),
        name="moe_combine",
    )(yg, h1, gates, g, b)


def _routing_tables(top_e, rank, counts):
    n_tok = top_e.shape[0]
    n_blocks = n_tok * TOP_K // FFN_TM
    n_vis = n_blocks + N_EXPERTS - 1
    end = jnp.cumsum(counts)
    start = end - counts
    onehot = top_e[:, :, None] == jnp.arange(N_EXPERTS, dtype=I32)[None, None, :]
    pos = (rank + jnp.sum(jnp.where(onehot, start[None, None, :], 0), axis=-1)).astype(I32)

    first = start // FFN_TM
    last = jnp.maximum(end - 1, 0) // FFN_TM
    nvis = jnp.where(counts > 0, last - first + 1, 0)
    vend = jnp.cumsum(nvis)
    vstart = vend - nvis
    total = vend[-1]
    v = jnp.arange(n_vis, dtype=I32)
    valid = v < total
    ev = jnp.sum((vend[None, :] <= jnp.minimum(v, total - 1)[:, None]).astype(I32), axis=1)
    ev = jnp.minimum(ev, N_EXPERTS - 1)
    pick = lambda tbl: jnp.sum(jnp.where(ev[:, None] == jnp.arange(N_EXPERTS, dtype=I32)[None, :],
                                         tbl[None, :], 0), axis=1)
    bv = jnp.where(valid, pick(first) + v - pick(vstart), n_blocks - 1).astype(I32)
    lo = jnp.where(valid, jnp.maximum(pick(start), bv * FFN_TM) - bv * FFN_TM, 0).astype(I32)
    hi = jnp.where(valid, jnp.minimum(pick(end), (bv + 1) * FFN_TM) - bv * FFN_TM, 0).astype(I32)
    prev_b = jnp.concatenate([jnp.full((1,), -1, I32), bv[:-1]])
    prev_e = jnp.concatenate([jnp.full((1,), -1, I32), ev[:-1]])
    fi = jnp.where(valid & (bv != prev_b), 1, 0).astype(I32)
    efi = jnp.where(ev != prev_e, 1, 0).astype(I32)
    return pos, (bv, ev.astype(I32), lo, hi, fi, efi)


def _moe(routed, layer, wgu, bgu, wdn, bdn, g, b):
    h1, h1p, te, gates, rk, cnt = routed
    n_tok = h1.shape[0]
    pos, visits = _routing_tables(te[:, :TOP_K], rk[:, :TOP_K], cnt[0, :N_EXPERTS].astype(I32))
    xs = _sc_scatter(h1p, pos)
    y = _ffn(visits, xs, layer, wgu, bgu, wdn, bdn)
    yg = _sc_gather(y, jnp.transpose(pos).reshape(-1)).reshape(TOP_K, n_tok, HALF)
    return _combine(yg, h1, gates, g, b)


def _row(a):
    return a.reshape(1, -1).astype(F32)


def _even_layer(h2, bsz, seq, w_in, q_norm, kv_norm, w_uq, w_uk, w_uv, w_iq, idx_g, idx_b, v_g, v_b,
                w_s, b_s, w_out, ln_g, ln_b, rw, rb):
    o4 = D_CQ + D_C + D_IDX + H_IDX
    w_in_p = jnp.concatenate(
        [w_in[:, :o4], jnp.zeros((D_MODEL, E_IN_PAD - E_IN_EVEN), F32), w_in[:, o4:]], axis=1)
    pad_idx = lambda a: jnp.pad(_row(a), ((0, 0), (0, LANES - D_IDX)))
    cq, ckv, kidx, widx, uv = _proj_even(h2, w_in_p.astype(BF16), _row(q_norm), _row(kv_norm),
                                         pad_idx(idx_g), pad_idx(idx_b))
    as3 = lambda a: a.reshape(bsz, seq, a.shape[-1])

    n_sel = min(TOPK_MAX, seq // 4)
    tq = min(DSA_TQ, seq)
    wuq = jnp.transpose(w_uq, (1, 0, 2)).astype(BF16)
    wiq = jnp.pad(jnp.transpose(w_iq, (1, 0, 2)),
                  ((0, 0), (0, 0), (0, LANES - D_IDX))).astype(BF16)
    wuk = w_uk.astype(BF16)
    eye = jnp.eye(H_A, dtype=F32)
    wuv = (w_uv[:, :, None, :] * eye[:, None, :, None]).reshape(H_A, D_C, H_A * DH_A).astype(BF16)
    n_a = H_A * DH_A
    assert tq == min(PROJ_TM, bsz * seq)
    o_a = [_dsa_tile(as3(cq), as3(ckv), as3(kidx), as3(widx), qt, tq, n_sel, wuq, wuk, wiq,
                     wuv).reshape(bsz * tq, n_a) for qt in range(seq // tq)]

    gbias = jnp.repeat(jnp.transpose(b_s), CG_B, axis=1)
    o_b = _gmlp(as3(uv), _row(v_g), _row(v_b), w_s, gbias)

    w_out_b = w_out.astype(BF16)
    return _outln([o_a, [o_b.reshape(bsz * seq, GMLP_WIDTH)]],
                  [w_out_b[:n_a], w_out_b[n_a:]], h2, _row(ln_g), _row(ln_b), rw, rb)


def _odd_layer(h2, bsz, seq, layer, w_in, lq1, lk1, lq2, lk2, subln_g, w_out, ln_g, ln_b, rw, rb):
    lam_init = 0.8 - 0.6 * math.exp(-0.3 * layer)
    qk, vt3 = _proj_odd(h2, w_in.astype(BF16), seq)
    qk3 = qk.reshape(bsz, seq, 2 * QK_W_C)
    tq = min(DIFF_TQ, seq)
    kpos = jnp.stack([_alibi_key_cols(seq, DH_C), _alibi_key_cols(seq, 0)])
    qa0 = _alibi_query_cols(H_C)
    qa = jnp.stack([jnp.roll(qa0, DH_C, axis=-1), qa0], axis=1)
    diag = jnp.transpose(_alibi_diag(H_C, tq), (0, 2, 1))
    g_col = subln_g.reshape(-1, 1).astype(F32)
    assert tq == min(PROJ_TM, bsz * seq)
    o = [_diff_tile(qk3, vt3, qt, tq, lam_init, kpos, qa, diag, _row(lq1), _row(lk1), _row(lq2),
                    _row(lk2), g_col).reshape(bsz * tq, QK_W_C) for qt in range(seq // tq)]
    return _outln([o], [w_out.astype(BF16)], h2, _row(ln_g), _row(ln_b), rw, rb)


def kernel(x, ev_w_in, ev_q_norm, ev_kv_norm, ev_w_uq, ev_w_uk, ev_w_uv, ev_w_iq, ev_idx_k_g, ev_idx_k_b, ev_v_norm_g, ev_v_norm_b, ev_w_s, ev_b_s, ev_w_out, od_w_in, od_lambda_q1, od_lambda_k1, od_lambda_q2, od_lambda_k2, od_subln_g, od_w_out, ln1_g, ln1_b, ln2_g, ln2_b, router_w, router_b, exp_w_gu, exp_b_gu, exp_w_dn, exp_b_dn):
    bsz, seq, d = x.shape
    h2 = x.reshape(bsz * seq, d)
    for l in range(DEPTH):
        j = l // 2
        rw = jnp.pad(router_w[l], ((0, 0), (0, LANES - N_EXPERTS)))
        rb = jnp.pad(router_b[l], (0, LANES - N_EXPERTS), constant_values=-jnp.inf).reshape(1, LANES)
        if l % 2 == 0:
            routed = _even_layer(h2, bsz, seq, ev_w_in[j], ev_q_norm[j], ev_kv_norm[j],
                                 ev_w_uq[j], ev_w_uk[j], ev_w_uv[j], ev_w_iq[j], ev_idx_k_g[j],
                                 ev_idx_k_b[j], ev_v_norm_g[j], ev_v_norm_b[j], ev_w_s[j],
                                 ev_b_s[j], ev_w_out[j], ln1_g[l], ln1_b[l], rw, rb)
        else:
            routed = _odd_layer(h2, bsz, seq, l, od_w_in[j], od_lambda_q1[j], od_lambda_k1[j],
                                od_lambda_q2[j], od_lambda_k2[j], od_subln_g[j], od_w_out[j],
                                ln1_g[l], ln1_b[l], rw, rb)
        h2 = _moe(routed, l, exp_w_gu, exp_b_gu[:, :, None, :], exp_w_dn, exp_b_dn[:, :, None, :],
                  _row(ln2_g[l]), _row(ln2_b[l]))
    return h2.reshape(bsz, seq, d)
```
